```python
import math
import jax, jax.numpy as jnp
from jax import lax
import numpy as np

D_MODEL = 1024
BATCH = 32
SEQ = 2048
DEPTH = 1

ATT_HEAD_DIM = 64
D_ATT = D_MODEL // 2
N_ATT_HEADS = D_ATT // ATT_HEAD_DIM
MOBA_BLOCK = 256
MOBA_TOPK = 3

SSM_HEAD_DIM = 64
D_SSM = 3 * D_MODEL // 2
N_SSM_HEADS = D_SSM // SSM_HEAD_DIM
N_SSM_GROUPS = 4
SSM_HEADS_PER_GROUP = N_SSM_HEADS // N_SSM_GROUPS
D_STATE = 128
SSM_CONV = 4
SSD_CHUNK = 128
D_XBC = D_SSM + 2 * N_SSM_GROUPS * D_STATE
D_MIX = D_ATT + D_SSM
D_IN_PROJ = 3 * D_ATT + D_SSM + D_XBC + N_SSM_HEADS

D_FF = ((8 * D_MODEL // 3 + 255) // 256) * 256
FFN_CONV = 3
EPS = 1e-6
NEG_INF = -1e30

kernel_name = 'moba_ssd_hybrid_block'


def rms_norm(x, g):
    xf = x.astype(jnp.float32)
    y = xf * lax.rsqrt(jnp.mean(xf * xf, axis=-1, keepdims=True) + EPS)
    return (y * g.astype(jnp.float32)).astype(x.dtype)


def causal_dwconv(x, w, b):
    k_width, ch = w.shape
    y = lax.conv_general_dilated(
        x, w[:, None, :].astype(x.dtype), window_strides=(1,),
        padding=[(k_width - 1, 0)], dimension_numbers=('NWC', 'WIO', 'NWC'),
        feature_group_count=ch)
    return y + b.astype(x.dtype)


def alibi_slopes(n_heads):
    return jnp.asarray(2.0 ** (-8.0 * np.arange(1, n_heads + 1) / n_heads), dtype=jnp.float32)


def moba_attention(q, k, v):
    H, S, DH = q.shape
    nb = S // MOBA_BLOCK
    n_extra = max(MOBA_TOPK - nb, 0)
    n_cand = nb + n_extra
    qb = q.reshape(H, nb, MOBA_BLOCK, DH)
    kb = k.reshape(H, nb, MOBA_BLOCK, DH)
    vb = v.reshape(H, nb, MOBA_BLOCK, DH)
    k_mean = jnp.mean(kb.astype(jnp.float32), axis=2)
    pad4 = ((0, 0), (0, n_extra), (0, 0), (0, 0))
    kb_c = jnp.pad(kb, pad4)
    vb_c = jnp.pad(vb, pad4)
    k_mean_c = jnp.pad(k_mean, ((0, 0), (0, n_extra), (0, 0)))
    slopes = alibi_slopes(H)
    scale = DH ** -0.5
    offs = jnp.arange(MOBA_BLOCK)
    own_dist = (offs[:, None] - offs[None, :]).astype(jnp.float32)
    own_causal = offs[:, None] >= offs[None, :]
    head_idx = jnp.arange(H)[:, None, None]
    cand = jnp.arange(n_cand)
    rank = jnp.arange(MOBA_TOPK)

    def query_block(i):
        qi = lax.dynamic_index_in_dim(qb, i, axis=1, keepdims=False)
        ki = lax.dynamic_index_in_dim(kb, i, axis=1, keepdims=False)
        vi = lax.dynamic_index_in_dim(vb, i, axis=1, keepdims=False)
        gate = jnp.einsum('hqd,hjd->hqj', qi.astype(jnp.float32), k_mean_c)
        gate = jnp.where(cand[None, None, :] < i, gate, NEG_INF)
        _, sel = lax.top_k(gate, MOBA_TOPK)
        kg = kb_c[head_idx, sel]
        vg = vb_c[head_idx, sel]
        s_own = jnp.einsum('hqd,hkd->hqk', qi, ki).astype(jnp.float32) * scale
        s_own = s_own - slopes[:, None, None] * own_dist
        s_own = jnp.where(own_causal[None], s_own, NEG_INF)
        q_pos = i * MOBA_BLOCK + offs
        k_pos = sel[..., None] * MOBA_BLOCK + offs
        dist = (q_pos[None, :, None, None] - k_pos).astype(jnp.float32)
        s_sel = jnp.einsum('hqd,hqrkd->hqrk', qi, kg).astype(jnp.float32) * scale
        s_sel = s_sel - slopes[:, None, None, None] * dist
        s_sel = jnp.where((rank < i)[None, None, :, None], s_sel, NEG_INF)
        scores = jnp.concatenate(
            [s_own, s_sel.reshape(H, MOBA_BLOCK, MOBA_TOPK * MOBA_BLOCK)], axis=-1)
        p = jax.nn.softmax(scores, axis=-1).astype(v.dtype)
        p_own = p[..., :MOBA_BLOCK]
        p_sel = p[..., MOBA_BLOCK:].reshape(H, MOBA_BLOCK, MOBA_TOPK, MOBA_BLOCK)
        return (jnp.einsum('hqk,hkd->hqd', p_own, vi)
                + jnp.einsum('hqrk,hqrkd->hqd', p_sel, vg))

    out = lax.map(query_block, jnp.arange(nb))
    return out.transpose(1, 0, 2, 3).reshape(H, S, DH)


def ssd_scan(xs, dt, a, bm, cm):
    Bsz, S, G, K, P = xs.shape
    N = bm.shape[-1]
    nc = S // SSD_CHUNK
    L = SSD_CHUNK
    x_dt = (xs.astype(jnp.float32) * dt[..., None]).reshape(Bsz, nc, L, G, K, P)
    bc = bm.astype(jnp.float32).reshape(Bsz, nc, L, G, N)
    cc = cm.astype(jnp.float32).reshape(Bsz, nc, L, G, N)
    a_dt = (dt * a).reshape(Bsz, nc, L, G, K).transpose(0, 1, 3, 4, 2)
    a_cs = jnp.cumsum(a_dt, axis=-1)
    causal = jnp.tril(jnp.ones((L, L), dtype=bool))
    seg = a_cs[..., :, None] - a_cs[..., None, :]
    decay = jnp.where(causal, jnp.exp(jnp.where(causal, seg, 0.0)), 0.0)
    cb = jnp.einsum('bclgn,bcsgn->bcgls', cc, bc)
    y_diag = jnp.einsum('bcgls,bcgkls,bcsgkp->bclgkp', cb, decay, x_dt)
    decay_states = jnp.exp(a_cs[..., -1:] - a_cs)
    states = jnp.einsum('bclgn,bcgkl,bclgkp->bcgkpn', bc, decay_states, x_dt)
    chunk_decay = jnp.exp(a_cs[..., -1])

    def step(h, inp):
        st, dec = inp
        return h * dec[..., None, None] + st, h

    h0 = jnp.zeros((Bsz, G, K, P, N), dtype=states.dtype)
    _, prev = lax.scan(step, h0, (jnp.moveaxis(states, 1, 0), jnp.moveaxis(chunk_decay, 1, 0)))
    prev = jnp.moveaxis(prev, 0, 1)
    y_off = jnp.einsum('bclgn,bcgkpn,bcgkl->bclgkp', cc, prev, jnp.exp(a_cs))
    return (y_diag + y_off).reshape(Bsz, S, G, K, P)


def hybrid_mixer(h, w_in, ssm_conv_w, ssm_conv_b, dt_bias, a_log, d_skip,
                 attn_norm_g, ssm_norm_g, w_out):
    Bsz, S, _ = h.shape
    s_pad = ((S + MOBA_BLOCK - 1) // MOBA_BLOCK) * MOBA_BLOCK
    hp = jnp.pad(h, ((0, 0), (0, s_pad - S), (0, 0)))
    proj = hp @ w_in.astype(h.dtype)
    cuts = [D_ATT, 2 * D_ATT, 3 * D_ATT, 3 * D_ATT + D_SSM, 3 * D_ATT + D_SSM + D_XBC]
    q, k, v, z, xbc, dt_raw = jnp.split(proj, cuts, axis=-1)

    def to_heads(t):
        return t.reshape(Bsz, s_pad, N_ATT_HEADS, ATT_HEAD_DIM).transpose(0, 2, 1, 3)

    att = lax.map(lambda qkv: moba_attention(*qkv), (to_heads(q), to_heads(k), to_heads(v)))
    att = att.transpose(0, 2, 1, 3).reshape(Bsz, s_pad, D_ATT)
    att = rms_norm(att, attn_norm_g)

    xbc = jax.nn.silu(causal_dwconv(xbc, ssm_conv_w, ssm_conv_b))
    xs, bm, cm = jnp.split(xbc, [D_SSM, D_SSM + N_SSM_GROUPS * D_STATE], axis=-1)
    dt = jax.nn.softplus(dt_raw.astype(jnp.float32) + dt_bias.astype(jnp.float32))
    a = -jnp.exp(a_log.astype(jnp.float32))
    xs5 = xs.reshape(Bsz, s_pad, N_SSM_GROUPS, SSM_HEADS_PER_GROUP, SSM_HEAD_DIM)
    y = ssd_scan(xs5,
                 dt.reshape(Bsz, s_pad, N_SSM_GROUPS, SSM_HEADS_PER_GROUP),
                 a.reshape(N_SSM_GROUPS, SSM_HEADS_PER_GROUP),
                 bm.reshape(Bsz, s_pad, N_SSM_GROUPS, D_STATE),
                 cm.reshape(Bsz, s_pad, N_SSM_GROUPS, D_STATE))
    y = y + d_skip.astype(jnp.float32).reshape(N_SSM_GROUPS, SSM_HEADS_PER_GROUP)[..., None] * xs5.astype(jnp.float32)
    y = y.reshape(Bsz, s_pad, D_SSM).astype(h.dtype) * jax.nn.silu(z)
    gsz = D_SSM // N_SSM_GROUPS
    y = rms_norm(y.reshape(Bsz, s_pad, N_SSM_GROUPS, gsz),
                 ssm_norm_g.reshape(N_SSM_GROUPS, gsz)).reshape(Bsz, s_pad, D_SSM)

    mixed = jnp.concatenate([att, y], axis=-1)[:, :S]
    return mixed @ w_out.astype(h.dtype)


def conv_ffn(h, w_up, ffn_conv_w, ffn_conv_b, w_down):
    u = h @ w_up.astype(h.dtype)
    u = causal_dwconv(u, ffn_conv_w, ffn_conv_b)
    gate, val = jnp.split(u, 2, axis=-1)
    return (jax.nn.silu(gate) * val) @ w_down.astype(h.dtype)


def setup_inputs(seed: int = 0) -> dict:
    key = jax.random.key(seed)
    ks = jax.random.split(key, 17)
    f32 = jnp.float32

    def nrm(k, shape, scale):
        return jax.random.normal(k, shape, f32) * scale

    u = jax.random.uniform(ks[5], (DEPTH, N_SSM_HEADS), f32)
    dt0 = jnp.exp(u * (math.log(0.1) - math.log(0.001)) + math.log(0.001))
    dt_bias = dt0 + jnp.log(-jnp.expm1(-dt0))
    a_log = jnp.log(jax.random.uniform(ks[6], (DEPTH, N_SSM_HEADS), f32, minval=1.0, maxval=16.0))
    return {
        'x': nrm(ks[0], (BATCH, SEQ, D_MODEL), 1.0),
        'ln1_g': 1.0 + nrm(ks[1], (DEPTH, D_MODEL), 0.02),
        'w_in': nrm(ks[2], (DEPTH, D_MODEL, D_IN_PROJ), D_MODEL ** -0.5),
        'ssm_conv_w': nrm(ks[3], (DEPTH, SSM_CONV, D_XBC), SSM_CONV ** -0.5),
        'ssm_conv_b': nrm(ks[4], (DEPTH, D_XBC), 0.02),
        'dt_bias': dt_bias,
        'a_log': a_log,
        'd_skip': 1.0 + nrm(ks[7], (DEPTH, N_SSM_HEADS), 0.1),
        'attn_norm_g': 1.0 + nrm(ks[8], (DEPTH, D_ATT), 0.02),
        'ssm_norm_g': 1.0 + nrm(ks[9], (DEPTH, D_SSM), 0.02),
        'w_out': nrm(ks[10], (DEPTH, D_MIX, D_MODEL), D_MIX ** -0.5),
        'ln2_g': 1.0 + nrm(ks[11], (DEPTH, D_MODEL), 0.02),
        'w_up': nrm(ks[12], (DEPTH, D_MODEL, 2 * D_FF), D_MODEL ** -0.5),
        'ffn_conv_w': nrm(ks[13], (DEPTH, FFN_CONV, 2 * D_FF), FFN_CONV ** -0.5),
        'ffn_conv_b': nrm(ks[14], (DEPTH, 2 * D_FF), 0.02),
        'w_down': nrm(ks[15], (DEPTH, D_FF, D_MODEL), D_FF ** -0.5),
        'lnf_g': 1.0 + nrm(ks[16], (D_MODEL,), 0.02),
    }


def reference(x, ln1_g, w_in, ssm_conv_w, ssm_conv_b, dt_bias, a_log, d_skip,
              attn_norm_g, ssm_norm_g, w_out, ln2_g, w_up, ffn_conv_w, ffn_conv_b,
              w_down, lnf_g):
    for l in range(DEPTH):
        x = x + hybrid_mixer(rms_norm(x, ln1_g[l]), w_in[l], ssm_conv_w[l], ssm_conv_b[l],
                             dt_bias[l], a_log[l], d_skip[l], attn_norm_g[l],
                             ssm_norm_g[l], w_out[l])
        x = x + conv_ffn(rms_norm(x, ln2_g[l]), w_up[l], ffn_conv_w[l], ffn_conv_b[l], w_down[l])
    return rms_norm(x, lnf_g)
```

```python
import functools

import jax
import jax.numpy as jnp
from jax import lax
from jax.experimental import pallas as pl
from jax.experimental.pallas import tpu as pltpu

F32 = jnp.float32
BF16 = jnp.bfloat16
HIGHEST = lax.Precision.HIGHEST

EPS = 1e-6
NEG_INF = -1e30

D_MODEL = 1024
HEAD_DIM = 64
D_ATT = 512
N_ATT_HEADS = D_ATT // HEAD_DIM
MOBA_BLOCK = 256
MOBA_TOPK = 3
D_SSM = 1536
N_SSM_GROUPS = 4
HEADS_PER_GROUP = 6
GROUP_ROWS = HEADS_PER_GROUP * HEAD_DIM
D_STATE = 128
SSM_CONV = 4
SSD_CHUNK = 128
D_FF = 2816
FFN_CONV = 3

LANES = 128
SUBLANES = 8
BF16_SUBLANES = 16
DT_ROWS_PER_GROUP = SUBLANES

F_Z = 0
F_XS = F_Z + D_SSM
F_Q = F_XS + D_SSM
F_V = F_Q + D_ATT
F_B = F_V + D_ATT
F_C = F_B + N_SSM_GROUPS * D_STATE
N_FEAT = F_C + N_SSM_GROUPS * D_STATE

NT_DIMS = (((1,), (1,)), ((), ()))

MIB = 1024 * 1024


def _sigmoid(x):
    return 1.0 / (1.0 + jnp.exp(-x))


def _rms_rows(x, gain):
    ms = jnp.mean(x * x, axis=-1, keepdims=True)
    return (x * lax.rsqrt(ms + EPS)) * gain


INPROJ_TM = 512
INPROJ_TN = 512


def _inproj_kernel(x_ref, g_ref, wk_ref, wf_ref, wdt_ref, k_ref, f_ref, dt_ref):
    h = _rms_rows(x_ref[...], g_ref[...]).astype(BF16)
    k_ref[...] = jnp.dot(h, wk_ref[...], preferred_element_type=F32).astype(BF16)
    for n0 in range(0, N_FEAT, INPROJ_TN):
        f_ref[n0:n0 + INPROJ_TN, :] = lax.dot_general(
            wf_ref[n0:n0 + INPROJ_TN, :], h, NT_DIMS, preferred_element_type=F32).astype(BF16)
    dt_ref[...] = lax.dot_general(wdt_ref[...], h, NT_DIMS, preferred_element_type=F32)


def _inproj(x2, ln_g, w_k, w_f_t, w_dt_t):
    tokens = x2.shape[0]
    tm = INPROJ_TM
    dt_rows = w_dt_t.shape[0]
    resident = dict(pipeline_mode=pl.Buffered(1))
    return pl.pallas_call(
        _inproj_kernel,
        grid=(tokens // tm,),
        in_specs=[
            pl.BlockSpec((tm, D_MODEL), lambda i: (i, 0)),
            pl.BlockSpec((1, D_MODEL), lambda i: (0, 0)),
            pl.BlockSpec((D_MODEL, D_ATT), lambda i: (0, 0), **resident),
            pl.BlockSpec((N_FEAT, D_MODEL), lambda i: (0, 0), **resident),
            pl.BlockSpec((dt_rows, D_MODEL), lambda i: (0, 0), **resident),
        ],
        out_specs=[
            pl.BlockSpec((tm, D_ATT), lambda i: (i, 0)),
            pl.BlockSpec((N_FEAT, tm), lambda i: (0, i)),
            pl.BlockSpec((dt_rows, tm), lambda i: (0, i)),
        ],
        out_shape=[
            jax.ShapeDtypeStruct((tokens, D_ATT), BF16),
            jax.ShapeDtypeStruct((N_FEAT, tokens), BF16),
            jax.ShapeDtypeStruct((dt_rows, tokens), F32),
        ],
        compiler_params=pltpu.CompilerParams(
            dimension_semantics=("arbitrary",), vmem_limit_bytes=48 * MIB),
        name="inproj",
    )(x2, ln_g, w_k, w_f_t, w_dt_t)


def _attn_kernel(q_ref, k_ref, v_ref, g_ref, o_ref, kmean_s, sel_s, ot_s, *, nb):
    i = pl.program_id(1)
    blk = MOBA_BLOCK
    pair = 2 * HEAD_DIM

    @pl.when(i == 0)
    def _():
        kmean_s[...] = jnp.zeros(kmean_s.shape, F32)
        for j in range(nb):
            kb = k_ref[j * blk:(j + 1) * blk, :].astype(F32)
            kmean_s[j:j + 1, :] = jnp.mean(kb, axis=0, keepdims=True)

    koff = lax.broadcasted_iota(jnp.int32, (blk, blk), 0)
    qoff = lax.broadcasted_iota(jnp.int32, (blk, blk), 1)
    rel = (koff - qoff).astype(F32)
    causal = koff <= qoff
    cand = lax.broadcasted_iota(jnp.int32, (kmean_s.shape[0], blk), 0)
    valid = cand < i
    feat = lax.broadcasted_iota(jnp.int32, (pair, blk), 0)

    for hp in range(N_ATT_HEADS // 2):
        q2 = q_ref[hp * pair:(hp + 1) * pair, :].astype(F32) * (HEAD_DIM ** -0.5)
        for sub in range(2):
            head = 2 * hp + sub
            slope = float(2.0 ** (-8.0 * (head + 1) / N_ATT_HEADS))
            in_head = (feat < HEAD_DIM) if sub == 0 else (feat >= HEAD_DIM)
            qm_f = jnp.where(in_head, q2, 0.0)
            qm = qm_f.astype(BF16)

            gate = jnp.dot(kmean_s[:, hp * pair:(hp + 1) * pair], qm_f,
                           precision=HIGHEST, preferred_element_type=F32)
            gate = jnp.where(valid, gate, NEG_INF)
            rank = jnp.zeros(gate.shape, F32)
            for jp in range(nb):
                gj = gate[jp:jp + 1, :]
                ge = jnp.where(gj >= gate, 1.0, 0.0)
                gt = jnp.where(gj > gate, 1.0, 0.0)
                rank = rank + jnp.where(cand > jp, ge, gt)
            sel_s[...] = jnp.where(valid, jnp.where(rank < MOBA_TOPK, 0.0, NEG_INF), NEG_INF)

            alibi = slope * rel

            def attend(j, carry, bias, qm=qm, head=head, hp=hp):
                m, l, acc = carry
                start = pl.multiple_of(j * blk, blk)
                kj = k_ref[pl.ds(start, blk), hp * pair:(hp + 1) * pair]
                s = jnp.dot(kj, qm, preferred_element_type=F32) + bias
                m_new = jnp.maximum(m, jnp.max(s, axis=0, keepdims=True))
                alpha = jnp.exp(m - m_new)
                p = jnp.exp(s - m_new)
                l = alpha * l + jnp.sum(p, axis=0, keepdims=True)
                vj = v_ref[head * HEAD_DIM:(head + 1) * HEAD_DIM, pl.ds(start, blk)]
                acc = alpha * acc + jnp.dot(vj, p.astype(BF16), preferred_element_type=F32)
                return m_new, l, acc

            init = (jnp.full((1, blk), NEG_INF, F32), jnp.zeros((1, blk), F32),
                    jnp.zeros((HEAD_DIM, blk), F32))
            carry = attend(i, init, jnp.where(causal, alibi, NEG_INF))

            def past(j, carry, alibi=alibi, slope=slope, attend=attend):
                row = sel_s[pl.ds(j, 1), :] + (slope * blk) * (j - i).astype(F32)
                return attend(j, carry, alibi + row)

            m, l, acc = lax.fori_loop(0, i, past, carry)
            ot_s[head * HEAD_DIM:(head + 1) * HEAD_DIM, :] = acc / l

    att = ot_s[...].T
    o_ref[...] = _rms_rows(att, g_ref[...]).astype(BF16)


def _attention(feat_t, k_rm, attn_g, batch, seq):
    nb = seq // MOBA_BLOCK
    cands = max(SUBLANES, nb)
    q_blk = F_Q // D_ATT
    v_blk = F_V // D_ATT
    return pl.pallas_call(
        functools.partial(_attn_kernel, nb=nb),
        grid=(batch, nb),
        in_specs=[
            pl.BlockSpec((D_ATT, MOBA_BLOCK), lambda b, i: (q_blk, b * nb + i)),
            pl.BlockSpec((seq, D_ATT), lambda b, i: (b, 0)),
            pl.BlockSpec((D_ATT, seq), lambda b, i: (v_blk, b)),
            pl.BlockSpec((1, D_ATT), lambda b, i: (0, 0)),
        ],
        out_specs=pl.BlockSpec((MOBA_BLOCK, D_ATT), lambda b, i: (b * nb + i, 0)),
        out_shape=jax.ShapeDtypeStruct((batch * seq, D_ATT), BF16),
        scratch_shapes=[
            pltpu.VMEM((cands, D_ATT), F32),
            pltpu.VMEM((cands, MOBA_BLOCK), F32),
            pltpu.VMEM((D_ATT, MOBA_BLOCK), F32),
        ],
        compiler_params=pltpu.CompilerParams(
            dimension_semantics=("arbitrary", "arbitrary"), vmem_limit_bytes=32 * MIB),
        name="moba_attention",
    )(feat_t, k_rm, feat_t, attn_g)


P_TAP0, P_BIAS, P_DSKIP, P_NORMG = 0, SSM_CONV, SSM_CONV + 1, SSM_CONV + 2
N_XS_PARAMS = SSM_CONV + 3
N_BC_PARAMS = SSM_CONV + 1


def _ssd_kernel(z_ref, xs_ref, b_ref, c_ref, dt_ref, pxs_ref, pb_ref, pc_ref, pdt_ref,
                y_ref, state_s, *, seq):
    L = SSD_CHUNK
    lane = lax.broadcasted_iota(jnp.int32, (1, L), 1)
    r_i = lax.broadcasted_iota(jnp.int32, (L, L), 0)
    c_i = lax.broadcasted_iota(jnp.int32, (L, L), 1)
    upper = jnp.where(r_i <= c_i, 1.0, 0.0).astype(F32)
    lower = jnp.where(c_i <= r_i, 1.0, 0.0).astype(F32)
    ones = jnp.ones((L, L), F32)
    s_le_l = r_i <= c_i

    dt_bias = pdt_ref[0]
    a_neg = -jnp.exp(pdt_ref[1])

    state_s[...] = jnp.zeros(state_s.shape, F32)

    def conv_silu(src_ref, p_ref, off, off_prev, first):
        xc = src_ref[:, pl.ds(off, L)].astype(F32)
        xp = src_ref[:, pl.ds(off_prev, L)].astype(F32)
        xp = jnp.where(first, jnp.zeros_like(xp), xp)
        acc = p_ref[P_BIAS] + p_ref[P_TAP0 + SSM_CONV - 1] * xc
        for k in range(1, SSM_CONV):
            mix = jnp.where(lane >= L - k, xp, xc)
            acc = acc + p_ref[P_TAP0 + SSM_CONV - 1 - k] * pltpu.roll(mix, k, axis=1)
        return acc * _sigmoid(acc)

    def chunk(c, _):
        off = pl.multiple_of(c * L, L)
        off_prev = pl.multiple_of(jnp.maximum(c - 1, 0) * L, L)
        first = c == 0
        xs = conv_silu(xs_ref, pxs_ref, off, off_prev, first)
        bm = conv_silu(b_ref, pb_ref, off, off_prev, first)
        cm = conv_silu(c_ref, pc_ref, off, off_prev, first).astype(BF16)

        raw = dt_ref[:, pl.ds(off, L)] + dt_bias
        dt = jnp.maximum(raw, 0.0) + jnp.log1p(jnp.exp(-jnp.abs(raw)))
        a_dt = dt * a_neg
        acs_row = jnp.dot(a_dt, upper, precision=HIGHEST, preferred_element_type=F32)
        acs_tot = jnp.dot(a_dt, ones, precision=HIGHEST, preferred_element_type=F32)
        acs_col = lax.dot_general(lower, a_dt, NT_DIMS, precision=HIGHEST,
                                  preferred_element_type=F32)

        b_t = bm.T.astype(BF16)
        cb_t = jnp.dot(b_t, cm, preferred_element_type=F32)

        y_diag, x_state, out_decay, carry_decay = [], [], [], []
        for k in range(HEADS_PER_GROUP):
            rows = slice(k * HEAD_DIM, (k + 1) * HEAD_DIM)
            x_dt = xs[rows, :] * dt[k:k + 1, :]
            seg = acs_row[k:k + 1, :] - acs_col[:, k:k + 1]
            decay = jnp.exp(jnp.where(s_le_l, seg, NEG_INF))
            m_t = (cb_t * decay).astype(BF16)
            y_diag.append(jnp.dot(x_dt.astype(BF16), m_t, preferred_element_type=F32))
            to_end = jnp.exp(acs_tot[k:k + 1, :] - acs_row[k:k + 1, :])
            x_state.append((x_dt * to_end).astype(BF16))
            out_decay.append(jnp.broadcast_to(jnp.exp(acs_row[k:k + 1, :]), (HEAD_DIM, L)))
            carry_decay.append(jnp.broadcast_to(jnp.exp(acs_tot[k:k + 1, :]), (HEAD_DIM, L)))
        y_diag = jnp.concatenate(y_diag, axis=0)
        x_state = jnp.concatenate(x_state, axis=0)
        out_decay = jnp.concatenate(out_decay, axis=0)
        carry_decay = jnp.concatenate(carry_decay, axis=0)

        prev = state_s[...]
        y_off = jnp.dot(prev.astype(BF16), cm, preferred_element_type=F32) * out_decay
        state_s[...] = prev * carry_decay + jnp.dot(x_state, b_t, preferred_element_type=F32)

        y = y_diag + y_off + pxs_ref[P_DSKIP] * xs
        zc = z_ref[:, pl.ds(off, L)].astype(F32)
        y = y * (zc * _sigmoid(zc))
        ms = jnp.mean(y * y, axis=0, keepdims=True)
        yn = (y * lax.rsqrt(ms + EPS)) * pxs_ref[P_NORMG]
        y_ref[pl.ds(off, L), :] = yn.T.astype(BF16)
        return 0

    lax.fori_loop(0, seq // L, chunk, 0)


def _ssd(feat_t, dt_t, pxs, pbc, pdt, batch, seq):
    g_rows = GROUP_ROWS
    z_blk0 = F_Z // g_rows
    xs_blk0 = F_XS // g_rows
    b_blk0 = F_B // D_STATE
    c_blk0 = F_C // D_STATE
    return pl.pallas_call(
        functools.partial(_ssd_kernel, seq=seq),
        grid=(batch, N_SSM_GROUPS),
        in_specs=[
            pl.BlockSpec((g_rows, seq), lambda b, g: (z_blk0 + g, b)),
            pl.BlockSpec((g_rows, seq), lambda b, g: (xs_blk0 + g, b)),
            pl.BlockSpec((D_STATE, seq), lambda b, g: (b_blk0 + g, b)),
            pl.BlockSpec((D_STATE, seq), lambda b, g: (c_blk0 + g, b)),
            pl.BlockSpec((DT_ROWS_PER_GROUP, seq), lambda b, g: (g, b)),
            pl.BlockSpec((N_XS_PARAMS, g_rows, LANES), lambda b, g: (0, g, 0)),
            pl.BlockSpec((N_BC_PARAMS, D_STATE, LANES), lambda b, g: (0, g, 0)),
            pl.BlockSpec((N_BC_PARAMS, D_STATE, LANES), lambda b, g: (0, N_SSM_GROUPS + g, 0)),
            pl.BlockSpec((2, DT_ROWS_PER_GROUP, LANES), lambda b, g: (0, g, 0)),
        ],
        out_specs=pl.BlockSpec((seq, g_rows), lambda b, g: (b, g)),
        out_shape=jax.ShapeDtypeStruct((batch * seq, D_SSM), BF16),
        scratch_shapes=[pltpu.VMEM((g_rows, D_STATE), F32)],
        compiler_params=pltpu.CompilerParams(
            dimension_semantics=("arbitrary", "arbitrary"), vmem_limit_bytes=32 * MIB),
        name="ssd_mixer",
    )(feat_t, feat_t, feat_t, feat_t, dt_t, pxs, pbc, pbc, pdt)


OUTPROJ_TM = 1024


def _outproj_kernel(x_ref, a_ref, y_ref, wa_ref, wy_ref, o_ref):
    o_ref[...] = (x_ref[...]
                  + jnp.dot(a_ref[...], wa_ref[...], preferred_element_type=F32)
                  + jnp.dot(y_ref[...], wy_ref[...], preferred_element_type=F32))


def _outproj(x2, att, y, w_att, w_ssm):
    tokens = x2.shape[0]
    tm = OUTPROJ_TM
    resident = dict(pipeline_mode=pl.Buffered(1))
    return pl.pallas_call(
        _outproj_kernel,
        grid=(tokens // tm,),
        in_specs=[
            pl.BlockSpec((tm, D_MODEL), lambda i: (i, 0)),
            pl.BlockSpec((tm, D_ATT), lambda i: (i, 0)),
            pl.BlockSpec((tm, D_SSM), lambda i: (i, 0)),
            pl.BlockSpec((D_ATT, D_MODEL), lambda i: (0, 0), **resident),
            pl.BlockSpec((D_SSM, D_MODEL), lambda i: (0, 0), **resident),
        ],
        out_specs=pl.BlockSpec((tm, D_MODEL), lambda i: (i, 0)),
        out_shape=jax.ShapeDtypeStruct((tokens, D_MODEL), F32),
        compiler_params=pltpu.CompilerParams(
            dimension_semantics=("arbitrary",), vmem_limit_bytes=40 * MIB),
        name="outproj",
    )(x2, att, y, w_att, w_ssm)


FFN_TM = 512
FFN_TN = 256
FFN_HALO = BF16_SUBLANES


def _ffn_kernel(x_ref, g2_ref, wup_ref, cw_ref, cb_ref, wdn_ref, gf_ref, o_ref,
                h_s, u_s, act_s, *, tiles_per_seq):
    i = pl.program_id(0)
    tm = FFN_TM
    halo = FFN_HALO

    @pl.when(i % tiles_per_seq == 0)
    def _():
        h_s[0:halo, :] = jnp.zeros((halo, D_MODEL), BF16)

    x = x_ref[...]
    h_s[halo:, :] = _rms_rows(x, g2_ref[...]).astype(BF16)
    h_ext = h_s[...]

    def conv(col0, width):
        u_s[...] = jnp.dot(h_ext, wup_ref[:, col0:col0 + width], preferred_element_type=F32)
        acc = cb_ref[:, col0:col0 + width]
        for k in range(FFN_CONV):
            shift = FFN_CONV - 1 - k
            acc = acc + cw_ref[k:k + 1, col0:col0 + width] * u_s[halo - shift:halo - shift + tm, :]
        return acc

    for n0 in range(0, D_FF, FFN_TN):
        gate = conv(n0, FFN_TN)
        val = conv(D_FF + n0, FFN_TN)
        act_s[:, n0:n0 + FFN_TN] = ((gate * _sigmoid(gate)) * val).astype(BF16)

    h_s[0:halo, :] = h_s[tm:tm + halo, :]
    x2 = x + jnp.dot(act_s[...], wdn_ref[...], preferred_element_type=F32)
    o_ref[...] = _rms_rows(x2, gf_ref[...])


def _ffn(x1, ln2_g, w_up, conv_w, conv_b, w_down, lnf_g, seq):
    tokens = x1.shape[0]
    tm = FFN_TM
    resident = dict(pipeline_mode=pl.Buffered(1))
    return pl.pallas_call(
        functools.partial(_ffn_kernel, tiles_per_seq=seq // tm),
        grid=(tokens // tm,),
        in_specs=[
            pl.BlockSpec((tm, D_MODEL), lambda i: (i, 0)),
            pl.BlockSpec((1, D_MODEL), lambda i: (0, 0)),
            pl.BlockSpec((D_MODEL, 2 * D_FF), lambda i: (0, 0), **resident),
            pl.BlockSpec((FFN_CONV, 2 * D_FF), lambda i: (0, 0)),
            pl.BlockSpec((1, 2 * D_FF), lambda i: (0, 0)),
            pl.BlockSpec((D_FF, D_MODEL), lambda i: (0, 0), **resident),
            pl.BlockSpec((1, D_MODEL), lambda i: (0, 0)),
        ],
        out_specs=pl.BlockSpec((tm, D_MODEL), lambda i: (i, 0)),
        out_shape=jax.ShapeDtypeStruct((tokens, D_MODEL), F32),
        scratch_shapes=[
            pltpu.VMEM((FFN_HALO + tm, D_MODEL), BF16),
            pltpu.VMEM((FFN_HALO + tm, FFN_TN), F32),
            pltpu.VMEM((tm, D_FF), BF16),
        ],
        compiler_params=pltpu.CompilerParams(
            dimension_semantics=("arbitrary",), vmem_limit_bytes=48 * MIB),
        name="conv_ffn",
    )(x1, ln2_g, w_up, conv_w, conv_b, w_down, lnf_g)


def _lane_bcast(v):
    return jnp.broadcast_to(v[..., None], v.shape + (LANES,)).astype(F32)


def _group_pad(v):
    v = v.reshape(N_SSM_GROUPS, HEADS_PER_GROUP)
    v = jnp.pad(v, ((0, 0), (0, DT_ROWS_PER_GROUP - HEADS_PER_GROUP)))
    return v.reshape(N_SSM_GROUPS * DT_ROWS_PER_GROUP)


def _layer(x2, batch, seq, ln1_g, w_in, ssm_conv_w, ssm_conv_b, dt_bias, a_log, d_skip,
           attn_norm_g, ssm_norm_g, w_out, ln2_g, w_up, ffn_conv_w, ffn_conv_b, w_down, out_g):
    c_q, c_k, c_v, c_z = 0, D_ATT, 2 * D_ATT, 3 * D_ATT
    c_xs = c_z + D_SSM
    c_b = c_xs + D_SSM
    c_c = c_b + N_SSM_GROUPS * D_STATE
    c_dt = c_c + N_SSM_GROUPS * D_STATE

    w_k = w_in[:, c_k:c_v].astype(BF16)
    w_f_t = jnp.concatenate(
        [w_in[:, c_z:c_xs], w_in[:, c_xs:c_b], w_in[:, c_q:c_k], w_in[:, c_v:c_z],
         w_in[:, c_b:c_c], w_in[:, c_c:c_dt]], axis=1).T.astype(BF16)
    w_dt_t = w_in[:, c_dt:].T.reshape(N_SSM_GROUPS, HEADS_PER_GROUP, D_MODEL)
    w_dt_t = jnp.pad(w_dt_t, ((0, 0), (0, DT_ROWS_PER_GROUP - HEADS_PER_GROUP), (0, 0)))
    w_dt_t = w_dt_t.reshape(N_SSM_GROUPS * DT_ROWS_PER_GROUP, D_MODEL).astype(BF16)

    k_rm, feat_t, dt_t = _inproj(x2, ln1_g[None, :], w_k, w_f_t, w_dt_t)

    att = _attention(feat_t, k_rm, attn_norm_g[None, :], batch, seq)

    d_skip_rows = jnp.repeat(d_skip, HEAD_DIM)
    pxs = _lane_bcast(jnp.concatenate(
        [ssm_conv_w[:, :D_SSM], ssm_conv_b[None, :D_SSM], d_skip_rows[None, :], ssm_norm_g[None, :]],
        axis=0))
    pbc = _lane_bcast(jnp.concatenate([ssm_conv_w[:, D_SSM:], ssm_conv_b[None, D_SSM:]], axis=0))
    pdt = _lane_bcast(jnp.stack([_group_pad(dt_bias), _group_pad(a_log)], axis=0))
    y = _ssd(feat_t, dt_t, pxs, pbc, pdt, batch, seq)

    w_out_b = w_out.astype(BF16)
    x1 = _outproj(x2, att, y, w_out_b[:D_ATT], w_out_b[D_ATT:])

    return _ffn(x1, ln2_g[None, :], w_up.astype(BF16), ffn_conv_w, ffn_conv_b[None, :],
                w_down.astype(BF16), out_g[None, :], seq)


def kernel(x, ln1_g, w_in, ssm_conv_w, ssm_conv_b, dt_bias, a_log, d_skip, attn_norm_g,
           ssm_norm_g, w_out, ln2_g, w_up, ffn_conv_w, ffn_conv_b, w_down, lnf_g):
    batch, seq, d_model = x.shape
    depth = ln1_g.shape[0]
    assert d_model == D_MODEL and depth == 1
    assert seq % MOBA_BLOCK == 0 and seq % FFN_TM == 0 and (batch * seq) % OUTPROJ_TM == 0
    x2 = x.reshape(batch * seq, d_model)
    out = _layer(x2, batch, seq, ln1_g[0], w_in[0], ssm_conv_w[0], ssm_conv_b[0], dt_bias[0],
                 a_log[0], d_skip[0], attn_norm_g[0], ssm_norm_g[0], w_out[0], ln2_g[0],
                 w_up[0], ffn_conv_w[0], ffn_conv_b[0], w_down[0], lnf_g)
    return out.reshape(batch, seq, d_model)
```

```python
import functools

import jax
import jax.numpy as jnp
from jax import lax
from jax.experimental import pallas as pl
from jax.experimental.pallas import tpu as pltpu

F32 = jnp.float32
BF16 = jnp.bfloat16
HIGHEST = lax.Precision.HIGHEST

EPS = 1e-6
NEG_INF = -1e30

D_MODEL = 1024
HEAD_DIM = 64
D_ATT = 512
N_ATT_HEADS = D_ATT // HEAD_DIM
MOBA_BLOCK = 256
MOBA_TOPK = 3
D_SSM = 1536
N_SSM_GROUPS = 4
HEADS_PER_GROUP = 6
GROUP_ROWS = HEADS_PER_GROUP * HEAD_DIM
D_STATE = 128
SSM_CONV = 4
SSD_CHUNK = 128
D_FF = 2816
FFN_CONV = 3

LANES = 128
SUBLANES = 8
BF16_SUBLANES = 16
DT_ROWS_PER_GROUP = SUBLANES

F_Z = 0
F_XS = F_Z + D_SSM
F_Q = F_XS + D_SSM
F_V = F_Q + D_ATT
F_B = F_V + D_ATT
F_C = F_B + N_SSM_GROUPS * D_STATE
N_FEAT = F_C + N_SSM_GROUPS * D_STATE

NT_DIMS = (((1,), (1,)), ((), ()))

MIB = 1024 * 1024


def _sigmoid(x):
    return 1.0 / (1.0 + jnp.exp(-x))


def _rms_rows(x, gain):
    ms = jnp.mean(x * x, axis=-1, keepdims=True)
    return (x * lax.rsqrt(ms + EPS)) * gain


INPROJ_TM = 512
INPROJ_TN = 512


def _inproj_kernel(x_ref, g_ref, wk_ref, wf_ref, wdt_ref, k_ref, f_ref, dt_ref):
    h = _rms_rows(x_ref[...], g_ref[...]).astype(BF16)
    k_ref[...] = jnp.dot(h, wk_ref[...], preferred_element_type=F32).astype(BF16)
    for n0 in range(0, N_FEAT, INPROJ_TN):
        f_ref[n0:n0 + INPROJ_TN, :] = lax.dot_general(
            wf_ref[n0:n0 + INPROJ_TN, :], h, NT_DIMS, preferred_element_type=F32).astype(BF16)
    dt_ref[...] = lax.dot_general(wdt_ref[...], h, NT_DIMS, preferred_element_type=F32)


def _inproj(x2, ln_g, w_k, w_f_t, w_dt_t):
    tokens = x2.shape[0]
    tm = INPROJ_TM
    dt_rows = w_dt_t.shape[0]
    resident = dict(pipeline_mode=pl.Buffered(1))
    return pl.pallas_call(
        _inproj_kernel,
        grid=(tokens // tm,),
        in_specs=[
            pl.BlockSpec((tm, D_MODEL), lambda i: (i, 0)),
            pl.BlockSpec((1, D_MODEL), lambda i: (0, 0)),
            pl.BlockSpec((D_MODEL, D_ATT), lambda i: (0, 0), **resident),
            pl.BlockSpec((N_FEAT, D_MODEL), lambda i: (0, 0), **resident),
            pl.BlockSpec((dt_rows, D_MODEL), lambda i: (0, 0), **resident),
        ],
        out_specs=[
            pl.BlockSpec((tm, D_ATT), lambda i: (i, 0)),
            pl.BlockSpec((N_FEAT, tm), lambda i: (0, i)),
            pl.BlockSpec((dt_rows, tm), lambda i: (0, i)),
        ],
        out_shape=[
            jax.ShapeDtypeStruct((tokens, D_ATT), BF16),
            jax.ShapeDtypeStruct((N_FEAT, tokens), BF16),
            jax.ShapeDtypeStruct((dt_rows, tokens), F32),
        ],
        compiler_params=pltpu.CompilerParams(
            dimension_semantics=("arbitrary",), vmem_limit_bytes=48 * MIB),
        name="inproj",
    )(x2, ln_g, w_k, w_f_t, w_dt_t)


def _alibi_slope(head):
    return float(2.0 ** (-8.0 * (head + 1) / N_ATT_HEADS))


def _attn_kernel(q_ref, k_ref, v_ref, g_ref, o_ref,
                 kmean_s, qm_s, sel_s, alibi_s, t_s, tmax_s, m_s, l_s, acc_s, *, nb):
    i = pl.program_id(1)
    blk = MOBA_BLOCK
    pair = 2 * HEAD_DIM

    @pl.when(i == 0)
    def _():
        kmean_s[...] = jnp.zeros(kmean_s.shape, F32)
        for j in range(nb):
            kb = k_ref[j * blk:(j + 1) * blk, :].astype(F32)
            kmean_s[j:j + 1, :] = jnp.mean(kb, axis=0, keepdims=True)
        key_off = lax.broadcasted_iota(jnp.int32, (blk, blk), 0).astype(F32)
        for head in range(N_ATT_HEADS):
            alibi_s[head] = _alibi_slope(head) * key_off

    cand = lax.broadcasted_iota(jnp.int32, (kmean_s.shape[0], blk), 0)
    valid = cand < i
    feat = lax.broadcasted_iota(jnp.int32, (pair, blk), 0)

    for hp in range(N_ATT_HEADS // 2):
        q2 = q_ref[hp * pair:(hp + 1) * pair, :].astype(F32) * (HEAD_DIM ** -0.5)
        for sub in range(2):
            head = 2 * hp + sub
            in_head = (feat < HEAD_DIM) if sub == 0 else (feat >= HEAD_DIM)
            qm_f = jnp.where(in_head, q2, 0.0)
            qm_s[head] = qm_f.astype(BF16)

            gate = jnp.dot(kmean_s[:, hp * pair:(hp + 1) * pair], qm_f,
                           precision=HIGHEST, preferred_element_type=F32)
            gate = jnp.where(valid, gate, NEG_INF)
            rank = jnp.zeros(gate.shape, F32)
            for jp in range(nb):
                gj = gate[jp:jp + 1, :]
                ge = jnp.where(gj >= gate, 1.0, 0.0)
                gt = jnp.where(gj > gate, 1.0, 0.0)
                rank = rank + jnp.where(cand > jp, ge, gt)
            sel_s[head] = jnp.where(cand == i, 0.0,
                                    jnp.where(valid, jnp.where(rank < MOBA_TOPK, 0.0, NEG_INF), NEG_INF))

    m_s[...] = jnp.full(m_s.shape, NEG_INF, F32)
    l_s[...] = jnp.zeros(l_s.shape, F32)
    acc_s[...] = jnp.zeros(acc_s.shape, F32)

    causal = (lax.broadcasted_iota(jnp.int32, (blk, blk), 0)
              <= lax.broadcasted_iota(jnp.int32, (blk, blk), 1))

    def scores(jt, slot, own):
        start = pl.multiple_of(jt * blk, blk)
        for hp in range(N_ATT_HEADS // 2):
            kj = k_ref[pl.ds(start, blk), hp * pair:(hp + 1) * pair]
            for sub in range(2):
                head = 2 * hp + sub
                t = jnp.dot(kj, qm_s[head], preferred_element_type=F32) + alibi_s[head]
                if own:
                    t = jnp.where(causal, t, NEG_INF)
                t_s[slot, head] = t
                tmax_s[slot, head, 0:1, :] = jnp.max(t, axis=0, keepdims=True)

    def accumulate(jt, slot):
        start = pl.multiple_of(jt * blk, blk)
        block_shift = (jt - i).astype(F32) * blk
        for head in range(N_ATT_HEADS):
            rows = slice(head * HEAD_DIM, (head + 1) * HEAD_DIM)
            row = sel_s[head, pl.ds(jt, 1), :] + _alibi_slope(head) * block_shift
            m_old = m_s[head, 0:1, :]
            m_new = jnp.maximum(m_old, tmax_s[slot, head, 0:1, :] + row)
            p = jnp.exp(t_s[slot, head] - (m_new - row))
            alpha = jnp.exp(m_old - m_new)
            l_s[head, 0:1, :] = alpha * l_s[head, 0:1, :] + jnp.sum(p, axis=0, keepdims=True)
            m_s[head, 0:1, :] = m_new
            vj = v_ref[rows, pl.ds(start, blk)]
            acc_s[rows, :] = alpha * acc_s[rows, :] + jnp.dot(
                vj, p.astype(BF16), preferred_element_type=F32)

    scores(i, 0, True)

    def pipelined(j, carry):
        scores(j, (j + 1) & 1, False)
        accumulate(jnp.where(j == 0, i, j - 1), j & 1)
        return carry

    lax.fori_loop(0, i, pipelined, 0)
    accumulate(jnp.maximum(i - 1, 0), i & 1)

    for head in range(N_ATT_HEADS):
        rows = slice(head * HEAD_DIM, (head + 1) * HEAD_DIM)
        acc_s[rows, :] = acc_s[rows, :] / l_s[head, 0:1, :]
    att = acc_s[...].T
    o_ref[...] = _rms_rows(att, g_ref[...]).astype(BF16)


def _attention(feat_t, k_rm, attn_g, batch, seq):
    nb = seq // MOBA_BLOCK
    cands = max(SUBLANES, nb)
    q_blk = F_Q // D_ATT
    v_blk = F_V // D_ATT
    return pl.pallas_call(
        functools.partial(_attn_kernel, nb=nb),
        grid=(batch, nb),
        in_specs=[
            pl.BlockSpec((D_ATT, MOBA_BLOCK), lambda b, i: (q_blk, b * nb + i)),
            pl.BlockSpec((seq, D_ATT), lambda b, i: (b, 0)),
            pl.BlockSpec((D_ATT, seq), lambda b, i: (v_blk, b)),
            pl.BlockSpec((1, D_ATT), lambda b, i: (0, 0)),
        ],
        out_specs=pl.BlockSpec((MOBA_BLOCK, D_ATT), lambda b, i: (b * nb + i, 0)),
        out_shape=jax.ShapeDtypeStruct((batch * seq, D_ATT), BF16),
        scratch_shapes=[
            pltpu.VMEM((cands, D_ATT), F32),
            pltpu.VMEM((N_ATT_HEADS, 2 * HEAD_DIM, MOBA_BLOCK), BF16),
            pltpu.VMEM((N_ATT_HEADS, cands, MOBA_BLOCK), F32),
            pltpu.VMEM((N_ATT_HEADS, MOBA_BLOCK, MOBA_BLOCK), F32),
            pltpu.VMEM((2, N_ATT_HEADS, MOBA_BLOCK, MOBA_BLOCK), F32),
            pltpu.VMEM((2, N_ATT_HEADS, SUBLANES, MOBA_BLOCK), F32),
            pltpu.VMEM((N_ATT_HEADS, SUBLANES, MOBA_BLOCK), F32),
            pltpu.VMEM((N_ATT_HEADS, SUBLANES, MOBA_BLOCK), F32),
            pltpu.VMEM((D_ATT, MOBA_BLOCK), F32),
        ],
        compiler_params=pltpu.CompilerParams(
            dimension_semantics=("arbitrary", "arbitrary"), vmem_limit_bytes=32 * MIB),
        name="moba_attention",
    )(feat_t, k_rm, feat_t, attn_g)


P_TAP0, P_BIAS, P_DSKIP, P_NORMG = 0, SSM_CONV, SSM_CONV + 1, SSM_CONV + 2
N_XS_PARAMS = SSM_CONV + 3
N_BC_PARAMS = SSM_CONV + 1


def _ssd_kernel(z_ref, xs_ref, b_ref, c_ref, dt_ref, pxs_ref, pb_ref, pc_ref, pdt_ref,
                y_ref, state_s, *, seq):
    L = SSD_CHUNK
    lane = lax.broadcasted_iota(jnp.int32, (1, L), 1)
    r_i = lax.broadcasted_iota(jnp.int32, (L, L), 0)
    c_i = lax.broadcasted_iota(jnp.int32, (L, L), 1)
    upper = jnp.where(r_i <= c_i, 1.0, 0.0).astype(F32)
    lower = jnp.where(c_i <= r_i, 1.0, 0.0).astype(F32)
    ones = jnp.ones((L, L), F32)
    s_le_l = r_i <= c_i

    dt_bias = pdt_ref[0]
    a_neg = -jnp.exp(pdt_ref[1])

    state_s[...] = jnp.zeros(state_s.shape, F32)

    def conv_silu(src_ref, p_ref, off, off_prev, first):
        xc = src_ref[:, pl.ds(off, L)].astype(F32)
        xp = src_ref[:, pl.ds(off_prev, L)].astype(F32)
        xp = jnp.where(first, jnp.zeros_like(xp), xp)
        acc = p_ref[P_BIAS] + p_ref[P_TAP0 + SSM_CONV - 1] * xc
        for k in range(1, SSM_CONV):
            mix = jnp.where(lane >= L - k, xp, xc)
            acc = acc + p_ref[P_TAP0 + SSM_CONV - 1 - k] * pltpu.roll(mix, k, axis=1)
        return acc * _sigmoid(acc)

    def chunk(c, _):
        off = pl.multiple_of(c * L, L)
        off_prev = pl.multiple_of(jnp.maximum(c - 1, 0) * L, L)
        first = c == 0
        xs = conv_silu(xs_ref, pxs_ref, off, off_prev, first)
        bm = conv_silu(b_ref, pb_ref, off, off_prev, first)
        cm = conv_silu(c_ref, pc_ref, off, off_prev, first).astype(BF16)

        raw = dt_ref[:, pl.ds(off, L)] + dt_bias
        dt = jnp.maximum(raw, 0.0) + jnp.log1p(jnp.exp(-jnp.abs(raw)))
        a_dt = dt * a_neg
        acs_row = jnp.dot(a_dt, upper, precision=HIGHEST, preferred_element_type=F32)
        acs_tot = jnp.dot(a_dt, ones, precision=HIGHEST, preferred_element_type=F32)
        acs_col = lax.dot_general(lower, a_dt, NT_DIMS, precision=HIGHEST,
                                  preferred_element_type=F32)

        b_t = bm.T.astype(BF16)
        cb_t = jnp.dot(b_t, cm, preferred_element_type=F32)

        y_diag, x_state, out_decay, carry_decay = [], [], [], []
        for k in range(HEADS_PER_GROUP):
            rows = slice(k * HEAD_DIM, (k + 1) * HEAD_DIM)
            x_dt = xs[rows, :] * dt[k:k + 1, :]
            seg = acs_row[k:k + 1, :] - acs_col[:, k:k + 1]
            decay = jnp.exp(jnp.where(s_le_l, seg, NEG_INF))
            m_t = (cb_t * decay).astype(BF16)
            y_diag.append(jnp.dot(x_dt.astype(BF16), m_t, preferred_element_type=F32))
            to_end = jnp.exp(acs_tot[k:k + 1, :] - acs_row[k:k + 1, :])
            x_state.append((x_dt * to_end).astype(BF16))
            out_decay.append(jnp.broadcast_to(jnp.exp(acs_row[k:k + 1, :]), (HEAD_DIM, L)))
            carry_decay.append(jnp.broadcast_to(jnp.exp(acs_tot[k:k + 1, :]), (HEAD_DIM, L)))
        y_diag = jnp.concatenate(y_diag, axis=0)
        x_state = jnp.concatenate(x_state, axis=0)
        out_decay = jnp.concatenate(out_decay, axis=0)
        carry_decay = jnp.concatenate(carry_decay, axis=0)

        prev = state_s[...]
        y_off = jnp.dot(prev.astype(BF16), cm, preferred_element_type=F32) * out_decay
        state_s[...] = prev * carry_decay + jnp.dot(x_state, b_t, preferred_element_type=F32)

        y = y_diag + y_off + pxs_ref[P_DSKIP] * xs
        zc = z_ref[:, pl.ds(off, L)].astype(F32)
        y = y * (zc * _sigmoid(zc))
        ms = jnp.mean(y * y, axis=0, keepdims=True)
        yn = (y * lax.rsqrt(ms + EPS)) * pxs_ref[P_NORMG]
        y_ref[pl.ds(off, L), :] = yn.T.astype(BF16)
        return 0

    lax.fori_loop(0, seq // L, chunk, 0)


def _ssd(feat_t, dt_t, pxs, pbc, pdt, batch, seq):
    g_rows = GROUP_ROWS
    z_blk0 = F_Z // g_rows
    xs_blk0 = F_XS // g_rows
    b_blk0 = F_B // D_STATE
    c_blk0 = F_C // D_STATE
    return pl.pallas_call(
        functools.partial(_ssd_kernel, seq=seq),
        grid=(batch, N_SSM_GROUPS),
        in_specs=[
            pl.BlockSpec((g_rows, seq), lambda b, g: (z_blk0 + g, b)),
            pl.BlockSpec((g_rows, seq), lambda b, g: (xs_blk0 + g, b)),
            pl.BlockSpec((D_STATE, seq), lambda b, g: (b_blk0 + g, b)),
            pl.BlockSpec((D_STATE, seq), lambda b, g: (c_blk0 + g, b)),
            pl.BlockSpec((DT_ROWS_PER_GROUP, seq), lambda b, g: (g, b)),
            pl.BlockSpec((N_XS_PARAMS, g_rows, LANES), lambda b, g: (0, g, 0)),
            pl.BlockSpec((N_BC_PARAMS, D_STATE, LANES), lambda b, g: (0, g, 0)),
            pl.BlockSpec((N_BC_PARAMS, D_STATE, LANES), lambda b, g: (0, N_SSM_GROUPS + g, 0)),
            pl.BlockSpec((2, DT_ROWS_PER_GROUP, LANES), lambda b, g: (0, g, 0)),
        ],
        out_specs=pl.BlockSpec((seq, g_rows), lambda b, g: (b, g)),
        out_shape=jax.ShapeDtypeStruct((batch * seq, D_SSM), BF16),
        scratch_shapes=[pltpu.VMEM((g_rows, D_STATE), F32)],
        compiler_params=pltpu.CompilerParams(
            dimension_semantics=("arbitrary", "arbitrary"), vmem_limit_bytes=32 * MIB),
        name="ssd_mixer",
    )(feat_t, feat_t, feat_t, feat_t, dt_t, pxs, pbc, pbc, pdt)


OUTPROJ_TM = 1024


def _outproj_kernel(x_ref, a_ref, y_ref, wa_ref, wy_ref, o_ref):
    o_ref[...] = (x_ref[...]
                  + jnp.dot(a_ref[...], wa_ref[...], preferred_element_type=F32)
                  + jnp.dot(y_ref[...], wy_ref[...], preferred_element_type=F32))


def _outproj(x2, att, y, w_att, w_ssm):
    tokens = x2.shape[0]
    tm = OUTPROJ_TM
    resident = dict(pipeline_mode=pl.Buffered(1))
    return pl.pallas_call(
        _outproj_kernel,
        grid=(tokens // tm,),
        in_specs=[
            pl.BlockSpec((tm, D_MODEL), lambda i: (i, 0)),
            pl.BlockSpec((tm, D_ATT), lambda i: (i, 0)),
            pl.BlockSpec((tm, D_SSM), lambda i: (i, 0)),
            pl.BlockSpec((D_ATT, D_MODEL), lambda i: (0, 0), **resident),
            pl.BlockSpec((D_SSM, D_MODEL), lambda i: (0, 0), **resident),
        ],
        out_specs=pl.BlockSpec((tm, D_MODEL), lambda i: (i, 0)),
        out_shape=jax.ShapeDtypeStruct((tokens, D_MODEL), F32),
        compiler_params=pltpu.CompilerParams(
            dimension_semantics=("arbitrary",), vmem_limit_bytes=40 * MIB),
        name="outproj",
    )(x2, att, y, w_att, w_ssm)


FFN_TM = 512
FFN_TN = 256
FFN_HALO = BF16_SUBLANES


def _ffn_kernel(x_ref, g2_ref, wup_ref, cw_ref, cb_ref, wdn_ref, gf_ref, o_ref,
                h_s, u_s, act_s, *, tiles_per_seq):
    i = pl.program_id(0)
    tm = FFN_TM
    halo = FFN_HALO

    @pl.when(i % tiles_per_seq == 0)
    def _():
        h_s[0:halo, :] = jnp.zeros((halo, D_MODEL), BF16)

    x = x_ref[...]
    h_s[halo:, :] = _rms_rows(x, g2_ref[...]).astype(BF16)
    h_ext = h_s[...]

    def conv(col0, width):
        u_s[...] = jnp.dot(h_ext, wup_ref[:, col0:col0 + width], preferred_element_type=F32)
        acc = cb_ref[:, col0:col0 + width]
        for k in range(FFN_CONV):
            shift = FFN_CONV - 1 - k
            acc = acc + cw_ref[k:k + 1, col0:col0 + width] * u_s[halo - shift:halo - shift + tm, :]
        return acc

    for n0 in range(0, D_FF, FFN_TN):
        gate = conv(n0, FFN_TN)
        val = conv(D_FF + n0, FFN_TN)
        act_s[:, n0:n0 + FFN_TN] = ((gate * _sigmoid(gate)) * val).astype(BF16)

    h_s[0:halo, :] = h_s[tm:tm + halo, :]
    x2 = x + jnp.dot(act_s[...], wdn_ref[...], preferred_element_type=F32)
    o_ref[...] = _rms_rows(x2, gf_ref[...])


def _ffn(x1, ln2_g, w_up, conv_w, conv_b, w_down, lnf_g, seq):
    tokens = x1.shape[0]
    tm = FFN_TM
    resident = dict(pipeline_mode=pl.Buffered(1))
    return pl.pallas_call(
        functools.partial(_ffn_kernel, tiles_per_seq=seq // tm),
        grid=(tokens // tm,),
        in_specs=[
            pl.BlockSpec((tm, D_MODEL), lambda i: (i, 0)),
            pl.BlockSpec((1, D_MODEL), lambda i: (0, 0)),
            pl.BlockSpec((D_MODEL, 2 * D_FF), lambda i: (0, 0), **resident),
            pl.BlockSpec((FFN_CONV, 2 * D_FF), lambda i: (0, 0)),
            pl.BlockSpec((1, 2 * D_FF), lambda i: (0, 0)),
            pl.BlockSpec((D_FF, D_MODEL), lambda i: (0, 0), **resident),
            pl.BlockSpec((1, D_MODEL), lambda i: (0, 0)),
        ],
        out_specs=pl.BlockSpec((tm, D_MODEL), lambda i: (i, 0)),
        out_shape=jax.ShapeDtypeStruct((tokens, D_MODEL), F32),
        scratch_shapes=[
            pltpu.VMEM((FFN_HALO + tm, D_MODEL), BF16),
            pltpu.VMEM((FFN_HALO + tm, FFN_TN), F32),
            pltpu.VMEM((tm, D_FF), BF16),
        ],
        compiler_params=pltpu.CompilerParams(
            dimension_semantics=("arbitrary",), vmem_limit_bytes=48 * MIB),
        name="conv_ffn",
    )(x1, ln2_g, w_up, conv_w, conv_b, w_down, lnf_g)


def _lane_bcast(v):
    return jnp.broadcast_to(v[..., None], v.shape + (LANES,)).astype(F32)


def _group_pad(v):
    v = v.reshape(N_SSM_GROUPS, HEADS_PER_GROUP)
    v = jnp.pad(v, ((0, 0), (0, DT_ROWS_PER_GROUP - HEADS_PER_GROUP)))
    return v.reshape(N_SSM_GROUPS * DT_ROWS_PER_GROUP)


def _layer(x2, batch, seq, ln1_g, w_in, ssm_conv_w, ssm_conv_b, dt_bias, a_log, d_skip,
           attn_norm_g, ssm_norm_g, w_out, ln2_g, w_up, ffn_conv_w, ffn_conv_b, w_down, out_g):
    c_q, c_k, c_v, c_z = 0, D_ATT, 2 * D_ATT, 3 * D_ATT
    c_xs = c_z + D_SSM
    c_b = c_xs + D_SSM
    c_c = c_b + N_SSM_GROUPS * D_STATE
    c_dt = c_c + N_SSM_GROUPS * D_STATE

    w_k = w_in[:, c_k:c_v].astype(BF16)
    w_f_t = jnp.concatenate(
        [w_in[:, c_z:c_xs], w_in[:, c_xs:c_b], w_in[:, c_q:c_k], w_in[:, c_v:c_z],
         w_in[:, c_b:c_c], w_in[:, c_c:c_dt]], axis=1).T.astype(BF16)
    w_dt_t = w_in[:, c_dt:].T.reshape(N_SSM_GROUPS, HEADS_PER_GROUP, D_MODEL)
    w_dt_t = jnp.pad(w_dt_t, ((0, 0), (0, DT_ROWS_PER_GROUP - HEADS_PER_GROUP), (0, 0)))
    w_dt_t = w_dt_t.reshape(N_SSM_GROUPS * DT_ROWS_PER_GROUP, D_MODEL).astype(BF16)

    k_rm, feat_t, dt_t = _inproj(x2, ln1_g[None, :], w_k, w_f_t, w_dt_t)

    att = _attention(feat_t, k_rm, attn_norm_g[None, :], batch, seq)

    d_skip_rows = jnp.repeat(d_skip, HEAD_DIM)
    pxs = _lane_bcast(jnp.concatenate(
        [ssm_conv_w[:, :D_SSM], ssm_conv_b[None, :D_SSM], d_skip_rows[None, :], ssm_norm_g[None, :]],
        axis=0))
    pbc = _lane_bcast(jnp.concatenate([ssm_conv_w[:, D_SSM:], ssm_conv_b[None, D_SSM:]], axis=0))
    pdt = _lane_bcast(jnp.stack([_group_pad(dt_bias), _group_pad(a_log)], axis=0))
    y = _ssd(feat_t, dt_t, pxs, pbc, pdt, batch, seq)

    w_out_b = w_out.astype(BF16)
    x1 = _outproj(x2, att, y, w_out_b[:D_ATT], w_out_b[D_ATT:])

    return _ffn(x1, ln2_g[None, :], w_up.astype(BF16), ffn_conv_w, ffn_conv_b[None, :],
                w_down.astype(BF16), out_g[None, :], seq)


def kernel(x, ln1_g, w_in, ssm_conv_w, ssm_conv_b, dt_bias, a_log, d_skip, attn_norm_g,
           ssm_norm_g, w_out, ln2_g, w_up, ffn_conv_w, ffn_conv_b, w_down, lnf_g):
    batch, seq, d_model = x.shape
    depth = ln1_g.shape[0]
    assert d_model == D_MODEL and depth == 1
    assert seq % MOBA_BLOCK == 0 and seq % FFN_TM == 0 and (batch * seq) % OUTPROJ_TM == 0
    x2 = x.reshape(batch * seq, d_model)
    out = _layer(x2, batch, seq, ln1_g[0], w_in[0], ssm_conv_w[0], ssm_conv_b[0], dt_bias[0],
                 a_log[0], d_skip[0], attn_norm_g[0], ssm_norm_g[0], w_out[0], ln2_g[0],
                 w_up[0], ffn_conv_w[0], ffn_conv_b[0], w_down[0], lnf_g)
    return out.reshape(batch, seq, d_model)
```

```python
import functools

import jax
import jax.numpy as jnp
from jax import lax
from jax.experimental import pallas as pl
from jax.experimental.pallas import tpu as pltpu

F32 = jnp.float32
BF16 = jnp.bfloat16
HIGHEST = lax.Precision.HIGHEST

EPS = 1e-6
NEG_INF = -1e30

D_MODEL = 1024
HEAD_DIM = 64
D_ATT = 512
N_ATT_HEADS = D_ATT // HEAD_DIM
MOBA_BLOCK = 256
MOBA_TOPK = 3
D_SSM = 1536
N_SSM_GROUPS = 4
HEADS_PER_GROUP = 6
GROUP_ROWS = HEADS_PER_GROUP * HEAD_DIM
D_STATE = 128
SSM_CONV = 4
SSD_CHUNK = 128
D_FF = 2816
FFN_CONV = 3

LANES = 128
SUBLANES = 8
BF16_SUBLANES = 16
DT_ROWS_PER_GROUP = SUBLANES

F_Z = 0
F_XS = F_Z + D_SSM
F_Q = F_XS + D_SSM
F_V = F_Q + D_ATT
F_B = F_V + D_ATT
F_C = F_B + N_SSM_GROUPS * D_STATE
N_FEAT = F_C + N_SSM_GROUPS * D_STATE

NT_DIMS = (((1,), (1,)), ((), ()))

MIB = 1024 * 1024


def _sigmoid(x):
    return 1.0 / (1.0 + jnp.exp(-x))


def _rms_rows(x, gain):
    ms = jnp.mean(x * x, axis=-1, keepdims=True)
    return (x * lax.rsqrt(ms + EPS)) * gain


INPROJ_TM = 512
INPROJ_TN = 512


def _conv_silu_time_on_lanes(cur, tail, cp_ref, ch0):
    rows, width = cur.shape
    lane = lax.broadcasted_iota(jnp.int32, (1, LANES), 1)
    taps = [cp_ref[k, ch0:ch0 + rows, :] for k in range(SSM_CONV)]
    bias = cp_ref[SSM_CONV, ch0:ch0 + rows, :]
    blocks = [tail] + [cur[:, c * LANES:(c + 1) * LANES] for c in range(width // LANES)]
    rolled = [[pltpu.roll(b, k, axis=1) for k in range(1, SSM_CONV)] for b in blocks]
    out = []
    for c in range(1, len(blocks)):
        acc = bias + taps[SSM_CONV - 1] * blocks[c]
        for k in range(1, SSM_CONV):
            shifted = jnp.where(lane >= k, rolled[c][k - 1], rolled[c - 1][k - 1])
            acc = acc + taps[SSM_CONV - 1 - k] * shifted
        out.append(acc * _sigmoid(acc))
    return jnp.concatenate(out, axis=1)


def _inproj_kernel(x_ref, g_ref, wk_ref, wf_ref, wdt_ref, cp_ref, k_ref, f_ref, dt_ref, tail_s,
                   *, tiles_per_seq):
    tn = INPROJ_TN
    tm = x_ref.shape[0]

    @pl.when(pl.program_id(0) % tiles_per_seq == 0)
    def _():
        tail_s[...] = jnp.zeros(tail_s.shape, F32)

    h = _rms_rows(x_ref[...], g_ref[...]).astype(BF16)
    k_ref[...] = jnp.dot(h, wk_ref[...], preferred_element_type=F32).astype(BF16)
    for n0 in range(0, N_FEAT, tn):
        res = lax.dot_general(wf_ref[n0:n0 + tn, :], h, NT_DIMS, preferred_element_type=F32)
        if n0 < F_XS:
            res = res * _sigmoid(res)
        elif n0 < F_Q or n0 >= F_B:
            ch0 = n0 - F_XS if n0 < F_Q else n0 - F_B + D_SSM
            tail = tail_s[ch0:ch0 + tn, :]
            tail_s[ch0:ch0 + tn, :] = res[:, tm - LANES:]
            res = _conv_silu_time_on_lanes(res, tail, cp_ref, ch0)
        f_ref[n0:n0 + tn, :] = res.astype(BF16)
    dt_ref[...] = lax.dot_general(wdt_ref[...], h, NT_DIMS, preferred_element_type=F32)


def _inproj(x2, ln_g, w_k, w_f_t, w_dt_t, conv_p, seq):
    tokens = x2.shape[0]
    tm = INPROJ_TM
    dt_rows = w_dt_t.shape[0]
    d_xbc = conv_p.shape[1]
    resident = dict(pipeline_mode=pl.Buffered(1))
    return pl.pallas_call(
        functools.partial(_inproj_kernel, tiles_per_seq=seq // tm),
        grid=(tokens // tm,),
        in_specs=[
            pl.BlockSpec((tm, D_MODEL), lambda i: (i, 0)),
            pl.BlockSpec((1, D_MODEL), lambda i: (0, 0)),
            pl.BlockSpec((D_MODEL, D_ATT), lambda i: (0, 0), **resident),
            pl.BlockSpec((N_FEAT, D_MODEL), lambda i: (0, 0), **resident),
            pl.BlockSpec((dt_rows, D_MODEL), lambda i: (0, 0), **resident),
            pl.BlockSpec((SSM_CONV + 1, d_xbc, LANES), lambda i: (0, 0, 0), **resident),
        ],
        out_specs=[
            pl.BlockSpec((tm, D_ATT), lambda i: (i, 0)),
            pl.BlockSpec((N_FEAT, tm), lambda i: (0, i)),
            pl.BlockSpec((dt_rows, tm), lambda i: (0, i)),
        ],
        out_shape=[
            jax.ShapeDtypeStruct((tokens, D_ATT), BF16),
            jax.ShapeDtypeStruct((N_FEAT, tokens), BF16),
            jax.ShapeDtypeStruct((dt_rows, tokens), F32),
        ],
        scratch_shapes=[pltpu.VMEM((d_xbc, LANES), F32)],
        compiler_params=pltpu.CompilerParams(
            dimension_semantics=("arbitrary",), vmem_limit_bytes=52 * MIB),
        name="inproj",
    )(x2, ln_g, w_k, w_f_t, w_dt_t, conv_p)


def _alibi_slope(head):
    return float(2.0 ** (-8.0 * (head + 1) / N_ATT_HEADS))


def _attn_kernel(q_ref, k_ref, kown_ref, v_ref, vown_ref, g_ref, o_ref,
                 kmean_s, qm_s, sel_s, alibi_s, t_s, tmax_s, m_s, acc_s, out_s, *, nb):
    i = pl.program_id(1)
    blk = MOBA_BLOCK
    pair = 2 * HEAD_DIM

    @pl.when(i == 0)
    def _():
        kmean_s[...] = jnp.zeros(kmean_s.shape, F32)
        for j in range(nb):
            kb = k_ref[j * blk:(j + 1) * blk, :].astype(F32)
            kmean_s[j:j + 1, :] = jnp.mean(kb, axis=0, keepdims=True)
        key_off = lax.broadcasted_iota(jnp.int32, (blk, blk), 0).astype(F32)
        for head in range(N_ATT_HEADS):
            alibi_s[head] = _alibi_slope(head) * key_off

    cand = lax.broadcasted_iota(jnp.int32, (kmean_s.shape[0], blk), 0)
    valid = cand < i
    feat = lax.broadcasted_iota(jnp.int32, (pair, blk), 0)

    for hp in range(N_ATT_HEADS // 2):
        q2 = q_ref[hp * pair:(hp + 1) * pair, :].astype(F32) * (HEAD_DIM ** -0.5)
        for sub in range(2):
            head = 2 * hp + sub
            in_head = (feat < HEAD_DIM) if sub == 0 else (feat >= HEAD_DIM)
            qm_f = jnp.where(in_head, q2, 0.0)
            qm_s[head] = qm_f.astype(BF16)

            gate = jnp.dot(kmean_s[:, hp * pair:(hp + 1) * pair], qm_f,
                           precision=HIGHEST, preferred_element_type=F32)
            gate = jnp.where(valid, gate, NEG_INF)
            rank = jnp.zeros(gate.shape, F32)
            for jp in range(nb):
                gj = gate[jp:jp + 1, :]
                ge = jnp.where(gj >= gate, 1.0, 0.0)
                gt = jnp.where(gj > gate, 1.0, 0.0)
                rank = rank + jnp.where(cand > jp, ge, gt)
            sel_s[head] = jnp.where(valid, jnp.where(rank < MOBA_TOPK, 0.0, NEG_INF), NEG_INF)

    m_s[...] = jnp.full(m_s.shape, NEG_INF, F32)
    acc_s[...] = jnp.zeros(acc_s.shape, F32)

    causal = (lax.broadcasted_iota(jnp.int32, (blk, blk), 0)
              <= lax.broadcasted_iota(jnp.int32, (blk, blk), 1))

    def scores(tile, slot):
        for hp in range(N_ATT_HEADS // 2):
            cols = slice(hp * pair, (hp + 1) * pair)
            kj = kown_ref[:, cols] if tile is None else k_ref[tile * blk:(tile + 1) * blk, cols]
            for sub in range(2):
                head = 2 * hp + sub
                t = jnp.dot(kj, qm_s[head], preferred_element_type=F32) + alibi_s[head]
                if tile is None:
                    t = jnp.where(causal, t, NEG_INF)
                t_s[slot, head] = t
                quarter = blk // 4
                tq = jnp.maximum(jnp.maximum(t[0:quarter], t[quarter:2 * quarter]),
                                 jnp.maximum(t[2 * quarter:3 * quarter], t[3 * quarter:]))
                tmax_s[slot, head, 0:1, :] = jnp.max(tq, axis=0, keepdims=True)

    ones_rows = jnp.ones((BF16_SUBLANES, blk), BF16)

    def accumulate(tile, slot):
        for head in range(N_ATT_HEADS):
            rows = slice(head * HEAD_DIM, (head + 1) * HEAD_DIM)
            m_old = m_s[head, 0:1, :]
            if tile is None:
                m_new = jnp.maximum(m_old, tmax_s[slot, head, 0:1, :])
                shift = m_new
                v = vown_ref[rows, :]
            else:
                row = sel_s[head, tile:tile + 1, :] + (_alibi_slope(head) * blk) * (tile - i).astype(F32)
                m_new = jnp.maximum(m_old, tmax_s[slot, head, 0:1, :] + row)
                shift = m_new - row
                v = v_ref[rows, tile * blk:(tile + 1) * blk]
            p = jnp.exp(t_s[slot, head] - shift)
            alpha = jnp.exp(m_old - m_new)
            m_s[head, 0:1, :] = m_new
            vj = jnp.concatenate([v, ones_rows], axis=0)
            acc_s[head] = alpha * acc_s[head] + jnp.dot(vj, p.astype(BF16), preferred_element_type=F32)

    scores(None, 0)
    for j in range(nb - 1):
        @pl.when(j < i)
        def _(j=j):
            scores(j, (j + 1) % 2)
            accumulate(None if j == 0 else j - 1, j % 2)

    @pl.when(i == 0)
    def _():
        accumulate(None, 0)

    for j in range(nb - 1):
        @pl.when(i == j + 1)
        def _(j=j):
            accumulate(j, (j + 1) % 2)

    for head in range(N_ATT_HEADS):
        out_s[head * HEAD_DIM:(head + 1) * HEAD_DIM, :] = (
            acc_s[head, 0:HEAD_DIM, :] / acc_s[head, HEAD_DIM:HEAD_DIM + 1, :])
    att = out_s[...].T
    o_ref[...] = _rms_rows(att, g_ref[...]).astype(BF16)


def _attention(feat_t, k_rm, attn_g, batch, seq):
    nb = seq // MOBA_BLOCK
    cands = max(SUBLANES, nb)
    q_blk = F_Q // D_ATT
    v_blk = F_V // D_ATT
    return pl.pallas_call(
        functools.partial(_attn_kernel, nb=nb),
        grid=(batch, nb),
        in_specs=[
            pl.BlockSpec((D_ATT, MOBA_BLOCK), lambda b, i: (q_blk, b * nb + i)),
            pl.BlockSpec((seq, D_ATT), lambda b, i: (b, 0)),
            pl.BlockSpec((MOBA_BLOCK, D_ATT), lambda b, i: (b * nb + i, 0)),
            pl.BlockSpec((D_ATT, seq), lambda b, i: (v_blk, b)),
            pl.BlockSpec((D_ATT, MOBA_BLOCK), lambda b, i: (v_blk, b * nb + i)),
            pl.BlockSpec((1, D_ATT), lambda b, i: (0, 0)),
        ],
        out_specs=pl.BlockSpec((MOBA_BLOCK, D_ATT), lambda b, i: (b * nb + i, 0)),
        out_shape=jax.ShapeDtypeStruct((batch * seq, D_ATT), BF16),
        scratch_shapes=[
            pltpu.VMEM((cands, D_ATT), F32),
            pltpu.VMEM((N_ATT_HEADS, 2 * HEAD_DIM, MOBA_BLOCK), BF16),
            pltpu.VMEM((N_ATT_HEADS, cands, MOBA_BLOCK), F32),
            pltpu.VMEM((N_ATT_HEADS, MOBA_BLOCK, MOBA_BLOCK), F32),
            pltpu.VMEM((2, N_ATT_HEADS, MOBA_BLOCK, MOBA_BLOCK), F32),
            pltpu.VMEM((2, N_ATT_HEADS, SUBLANES, MOBA_BLOCK), F32),
            pltpu.VMEM((N_ATT_HEADS, SUBLANES, MOBA_BLOCK), F32),
            pltpu.VMEM((N_ATT_HEADS, HEAD_DIM + BF16_SUBLANES, MOBA_BLOCK), F32),
            pltpu.VMEM((D_ATT, MOBA_BLOCK), F32),
        ],
        compiler_params=pltpu.CompilerParams(
            dimension_semantics=("arbitrary", "arbitrary"), vmem_limit_bytes=32 * MIB),
        name="moba_attention",
    )(feat_t, k_rm, k_rm, feat_t, feat_t, attn_g)


P_DSKIP, P_NORMG = 0, 1
N_XS_PARAMS = 2


def _ssd_kernel(zs_ref, xs_ref, b_ref, c_ref, dt_ref, pxs_ref, pdt_ref, y_ref, state_s, *, seq):
    L = SSD_CHUNK
    r_i = lax.broadcasted_iota(jnp.int32, (L, L), 0)
    c_i = lax.broadcasted_iota(jnp.int32, (L, L), 1)
    upper = jnp.where(r_i <= c_i, 1.0, 0.0).astype(F32)
    lower = jnp.where(c_i <= r_i, 1.0, 0.0).astype(F32)
    ones = jnp.ones((L, L), F32)
    s_le_l = r_i <= c_i

    dt_bias = pdt_ref[0]
    a_neg = -jnp.exp(pdt_ref[1])

    state_s[...] = jnp.zeros(state_s.shape, F32)

    def chunk(c):
        off = c * L
        xs = xs_ref[:, pl.ds(off, L)].astype(F32)
        cm = c_ref[:, pl.ds(off, L)]

        raw = dt_ref[:, pl.ds(off, L)] + dt_bias
        dt = jnp.maximum(raw, 0.0) + jnp.log1p(jnp.exp(-jnp.abs(raw)))
        a_dt = dt * a_neg
        acs_row = jnp.dot(a_dt, upper, precision=HIGHEST, preferred_element_type=F32)
        acs_tot = jnp.dot(a_dt, ones, precision=HIGHEST, preferred_element_type=F32)
        acs_col = lax.dot_general(lower, a_dt, NT_DIMS, precision=HIGHEST,
                                  preferred_element_type=F32)

        b_t = b_ref[:, pl.ds(off, L)].astype(F32).T.astype(BF16)
        cb_t = jnp.dot(b_t, cm, preferred_element_type=F32)

        y_diag, x_state, out_decay, carry_decay = [], [], [], []
        for k in range(HEADS_PER_GROUP):
            rows = slice(k * HEAD_DIM, (k + 1) * HEAD_DIM)
            x_dt = xs[rows, :] * dt[k:k + 1, :]
            seg = acs_row[k:k + 1, :] - acs_col[:, k:k + 1]
            decay = jnp.exp(jnp.where(s_le_l, seg, NEG_INF))
            m_t = (cb_t * decay).astype(BF16)
            y_diag.append(jnp.dot(x_dt.astype(BF16), m_t, preferred_element_type=F32))
            to_end = jnp.exp(acs_tot[k:k + 1, :] - acs_row[k:k + 1, :])
            x_state.append((x_dt * to_end).astype(BF16))
            out_decay.append(jnp.broadcast_to(jnp.exp(acs_row[k:k + 1, :]), (HEAD_DIM, L)))
            carry_decay.append(jnp.broadcast_to(jnp.exp(acs_tot[k:k + 1, :]), (HEAD_DIM, L)))
        y_diag = jnp.concatenate(y_diag, axis=0)
        x_state = jnp.concatenate(x_state, axis=0)
        out_decay = jnp.concatenate(out_decay, axis=0)
        carry_decay = jnp.concatenate(carry_decay, axis=0)

        prev = state_s[...]
        y_off = jnp.dot(prev.astype(BF16), cm, preferred_element_type=F32) * out_decay
        state_s[...] = prev * carry_decay + jnp.dot(x_state, b_t, preferred_element_type=F32)

        y = y_diag + y_off + pxs_ref[P_DSKIP] * xs
        y = y * zs_ref[:, pl.ds(off, L)].astype(F32)
        ms = jnp.mean(y * y, axis=0, keepdims=True)
        yn = (y * lax.rsqrt(ms + EPS)) * pxs_ref[P_NORMG]
        y_ref[pl.ds(off, L), :] = yn.T.astype(BF16)

    for c in range(seq // L):
        chunk(c)


def _ssd(feat_t, dt_t, pxs, pdt, batch, seq):
    g_rows = GROUP_ROWS
    z_blk0 = F_Z // g_rows
    xs_blk0 = F_XS // g_rows
    b_blk0 = F_B // D_STATE
    c_blk0 = F_C // D_STATE
    return pl.pallas_call(
        functools.partial(_ssd_kernel, seq=seq),
        grid=(batch, N_SSM_GROUPS),
        in_specs=[
            pl.BlockSpec((g_rows, seq), lambda b, g: (z_blk0 + g, b)),
            pl.BlockSpec((g_rows, seq), lambda b, g: (xs_blk0 + g, b)),
            pl.BlockSpec((D_STATE, seq), lambda b, g: (b_blk0 + g, b)),
            pl.BlockSpec((D_STATE, seq), lambda b, g: (c_blk0 + g, b)),
            pl.BlockSpec((DT_ROWS_PER_GROUP, seq), lambda b, g: (g, b)),
            pl.BlockSpec((N_XS_PARAMS, g_rows, LANES), lambda b, g: (0, g, 0)),
            pl.BlockSpec((2, DT_ROWS_PER_GROUP, LANES), lambda b, g: (0, g, 0)),
        ],
        out_specs=pl.BlockSpec((seq, g_rows), lambda b, g: (b, g)),
        out_shape=jax.ShapeDtypeStruct((batch * seq, D_SSM), BF16),
        scratch_shapes=[pltpu.VMEM((g_rows, D_STATE), F32)],
        compiler_params=pltpu.CompilerParams(
            dimension_semantics=("arbitrary", "arbitrary"), vmem_limit_bytes=32 * MIB),
        name="ssd_mixer",
    )(feat_t, feat_t, feat_t, feat_t, dt_t, pxs, pdt)


OUTPROJ_TM = 1024


def _outproj_kernel(x_ref, a_ref, y_ref, wa_ref, wy_ref, o_ref):
    o_ref[...] = (x_ref[...]
                  + jnp.dot(a_ref[...], wa_ref[...], preferred_element_type=F32)
                  + jnp.dot(y_ref[...], wy_ref[...], preferred_element_type=F32))


def _outproj(x2, att, y, w_att, w_ssm):
    tokens = x2.shape[0]
    tm = OUTPROJ_TM
    resident = dict(pipeline_mode=pl.Buffered(1))
    return pl.pallas_call(
        _outproj_kernel,
        grid=(tokens // tm,),
        in_specs=[
            pl.BlockSpec((tm, D_MODEL), lambda i: (i, 0)),
            pl.BlockSpec((tm, D_ATT), lambda i: (i, 0)),
            pl.BlockSpec((tm, D_SSM), lambda i: (i, 0)),
            pl.BlockSpec((D_ATT, D_MODEL), lambda i: (0, 0), **resident),
            pl.BlockSpec((D_SSM, D_MODEL), lambda i: (0, 0), **resident),
        ],
        out_specs=pl.BlockSpec((tm, D_MODEL), lambda i: (i, 0)),
        out_shape=jax.ShapeDtypeStruct((tokens, D_MODEL), F32),
        compiler_params=pltpu.CompilerParams(
            dimension_semantics=("arbitrary",), vmem_limit_bytes=40 * MIB),
        name="outproj",
    )(x2, att, y, w_att, w_ssm)


FFN_TM = 512
FFN_TN = 256
FFN_HALO = BF16_SUBLANES


def _ffn_kernel(x_ref, g2_ref, wup_ref, cw_ref, cb_ref, wdn_ref, gf_ref, o_ref,
                h_s, u_s, act_s, *, tiles_per_seq):
    i = pl.program_id(0)
    tm = FFN_TM
    halo = FFN_HALO

    @pl.when(i % tiles_per_seq == 0)
    def _():
        h_s[0:halo, :] = jnp.zeros((halo, D_MODEL), BF16)

    x = x_ref[...]
    h_s[halo:, :] = _rms_rows(x, g2_ref[...]).astype(BF16)
    h_ext = h_s[...]

    def conv(col0, width):
        u_s[...] = jnp.dot(h_ext, wup_ref[:, col0:col0 + width], preferred_element_type=F32)
        acc = cb_ref[:, col0:col0 + width]
        for k in range(FFN_CONV):
            shift = FFN_CONV - 1 - k
            acc = acc + cw_ref[k:k + 1, col0:col0 + width] * u_s[halo - shift:halo - shift + tm, :]
        return acc

    for n0 in range(0, D_FF, FFN_TN):
        gate = conv(n0, FFN_TN)
        val = conv(D_FF + n0, FFN_TN)
        act_s[:, n0:n0 + FFN_TN] = ((gate * _sigmoid(gate)) * val).astype(BF16)

    h_s[0:halo, :] = h_s[tm:tm + halo, :]
    x2 = x + jnp.dot(act_s[...], wdn_ref[...], preferred_element_type=F32)
    o_ref[...] = _rms_rows(x2, gf_ref[...])


def _ffn(x1, ln2_g, w_up, conv_w, conv_b, w_down, lnf_g, seq):
    tokens = x1.shape[0]
    tm = FFN_TM
    resident = dict(pipeline_mode=pl.Buffered(1))
    return pl.pallas_call(
        functools.partial(_ffn_kernel, tiles_per_seq=seq // tm),
        grid=(tokens // tm,),
        in_specs=[
            pl.BlockSpec((tm, D_MODEL), lambda i: (i, 0)),
            pl.BlockSpec((1, D_MODEL), lambda i: (0, 0)),
            pl.BlockSpec((D_MODEL, 2 * D_FF), lambda i: (0, 0), **resident),
            pl.BlockSpec((FFN_CONV, 2 * D_FF), lambda i: (0, 0)),
            pl.BlockSpec((1, 2 * D_FF), lambda i: (0, 0)),
            pl.BlockSpec((D_FF, D_MODEL), lambda i: (0, 0), **resident),
            pl.BlockSpec((1, D_MODEL), lambda i: (0, 0)),
        ],
        out_specs=pl.BlockSpec((tm, D_MODEL), lambda i: (i, 0)),
        out_shape=jax.ShapeDtypeStruct((tokens, D_MODEL), F32),
        scratch_shapes=[
            pltpu.VMEM((FFN_HALO + tm, D_MODEL), BF16),
            pltpu.VMEM((FFN_HALO + tm, FFN_TN), F32),
            pltpu.VMEM((tm, D_FF), BF16),
        ],
        compiler_params=pltpu.CompilerParams(
            dimension_semantics=("arbitrary",), vmem_limit_bytes=48 * MIB),
        name="conv_ffn",
    )(x1, ln2_g, w_up, conv_w, conv_b, w_down, lnf_g)


def _lane_bcast(v):
    return jnp.broadcast_to(v[..., None], v.shape + (LANES,)).astype(F32)


def _group_pad(v):
    v = v.reshape(N_SSM_GROUPS, HEADS_PER_GROUP)
    v = jnp.pad(v, ((0, 0), (0, DT_ROWS_PER_GROUP - HEADS_PER_GROUP)))
    return v.reshape(N_SSM_GROUPS * DT_ROWS_PER_GROUP)


def _layer(x2, batch, seq, ln1_g, w_in, ssm_conv_w, ssm_conv_b, dt_bias, a_log, d_skip,
           attn_norm_g, ssm_norm_g, w_out, ln2_g, w_up, ffn_conv_w, ffn_conv_b, w_down, out_g):
    c_q, c_k, c_v, c_z = 0, D_ATT, 2 * D_ATT, 3 * D_ATT
    c_xs = c_z + D_SSM
    c_b = c_xs + D_SSM
    c_c = c_b + N_SSM_GROUPS * D_STATE
    c_dt = c_c + N_SSM_GROUPS * D_STATE

    w_k = w_in[:, c_k:c_v].astype(BF16)
    w_f_t = jnp.concatenate(
        [w_in[:, c_z:c_xs], w_in[:, c_xs:c_b], w_in[:, c_q:c_k], w_in[:, c_v:c_z],
         w_in[:, c_b:c_c], w_in[:, c_c:c_dt]], axis=1).T.astype(BF16)
    w_dt_t = w_in[:, c_dt:].T.reshape(N_SSM_GROUPS, HEADS_PER_GROUP, D_MODEL)
    w_dt_t = jnp.pad(w_dt_t, ((0, 0), (0, DT_ROWS_PER_GROUP - HEADS_PER_GROUP), (0, 0)))
    w_dt_t = w_dt_t.reshape(N_SSM_GROUPS * DT_ROWS_PER_GROUP, D_MODEL).astype(BF16)

    conv_p = _lane_bcast(jnp.concatenate([ssm_conv_w, ssm_conv_b[None, :]], axis=0))
    k_rm, feat_t, dt_t = _inproj(x2, ln1_g[None, :], w_k, w_f_t, w_dt_t, conv_p, seq)

    att = _attention(feat_t, k_rm, attn_norm_g[None, :], batch, seq)

    pxs = _lane_bcast(jnp.stack([jnp.repeat(d_skip, HEAD_DIM), ssm_norm_g], axis=0))
    pdt = _lane_bcast(jnp.stack([_group_pad(dt_bias), _group_pad(a_log)], axis=0))
    y = _ssd(feat_t, dt_t, pxs, pdt, batch, seq)

    w_out_b = w_out.astype(BF16)
    x1 = _outproj(x2, att, y, w_out_b[:D_ATT], w_out_b[D_ATT:])

    return _ffn(x1, ln2_g[None, :], w_up.astype(BF16), ffn_conv_w, ffn_conv_b[None, :],
                w_down.astype(BF16), out_g[None, :], seq)


def kernel(x, ln1_g, w_in, ssm_conv_w, ssm_conv_b, dt_bias, a_log, d_skip, attn_norm_g,
           ssm_norm_g, w_out, ln2_g, w_up, ffn_conv_w, ffn_conv_b, w_down, lnf_g):
    batch, seq, d_model = x.shape
    depth = ln1_g.shape[0]
    assert d_model == D_MODEL and depth == 1
    assert seq % MOBA_BLOCK == 0 and seq % FFN_TM == 0 and seq % INPROJ_TM == 0
    assert (batch * seq) % OUTPROJ_TM == 0
    x2 = x.reshape(batch * seq, d_model)
    out = _layer(x2, batch, seq, ln1_g[0], w_in[0], ssm_conv_w[0], ssm_conv_b[0], dt_bias[0],
                 a_log[0], d_skip[0], attn_norm_g[0], ssm_norm_g[0], w_out[0], ln2_g[0],
                 w_up[0], ffn_conv_w[0], ffn_conv_b[0], w_down[0], lnf_g)
    return out.reshape(batch, seq, d_model)
```

```python
import functools

import jax
import jax.numpy as jnp
from jax import lax
from jax.experimental import pallas as pl
from jax.experimental.pallas import tpu as pltpu

F32 = jnp.float32
BF16 = jnp.bfloat16
HIGHEST = lax.Precision.HIGHEST

EPS = 1e-6
NEG_INF = -1e30

D_MODEL = 1024
HEAD_DIM = 64
D_ATT = 512
N_ATT_HEADS = D_ATT // HEAD_DIM
MOBA_BLOCK = 256
MOBA_TOPK = 3
D_SSM = 1536
N_SSM_GROUPS = 4
HEADS_PER_GROUP = 6
GROUP_ROWS = HEADS_PER_GROUP * HEAD_DIM
D_STATE = 128
SSM_CONV = 4
SSD_CHUNK = 128
D_FF = 2816
FFN_CONV = 3

LANES = 128
SUBLANES = 8
BF16_SUBLANES = 16
DT_ROWS_PER_GROUP = SUBLANES

F_Z = 0
F_XS = F_Z + D_SSM
F_Q = F_XS + D_SSM
F_V = F_Q + D_ATT
F_B = F_V + D_ATT
F_C = F_B + N_SSM_GROUPS * D_STATE
N_FEAT = F_C + N_SSM_GROUPS * D_STATE

NT_DIMS = (((1,), (1,)), ((), ()))

MIB = 1024 * 1024


def _sigmoid(x):
    return 1.0 / (1.0 + jnp.exp(-x))


def _rms_rows(x, gain):
    ms = jnp.mean(x * x, axis=-1, keepdims=True)
    return (x * lax.rsqrt(ms + EPS)) * gain


INPROJ_TM = 512
INPROJ_TN = 256


def _conv_silu_time_on_lanes(cur, tail_rolled, cp_ref, ch0):
    rows, width = cur.shape
    lane = lax.broadcasted_iota(jnp.int32, (1, LANES), 1)
    taps = [cp_ref[k, ch0:ch0 + rows, :] for k in range(SSM_CONV)]
    bias = cp_ref[SSM_CONV, ch0:ch0 + rows, :]
    blocks = [cur[:, c * LANES:(c + 1) * LANES] for c in range(width // LANES)]
    rolled = [tail_rolled] + [[pltpu.roll(b, k, axis=1) for k in range(1, SSM_CONV)] for b in blocks]
    out = []
    for c, block in enumerate(blocks):
        acc = bias + taps[SSM_CONV - 1] * block
        for k in range(1, SSM_CONV):
            shifted = jnp.where(lane >= k, rolled[c + 1][k - 1], rolled[c][k - 1])
            acc = acc + taps[SSM_CONV - 1 - k] * shifted
        out.append(acc * _sigmoid(acc))
    return jnp.concatenate(out, axis=1), rolled[-1]


def _inproj_kernel(x_ref, g_ref, wk_ref, wf_ref, wdt_ref, cp_ref, k_ref, f_ref, dt_ref, tail_s,
                   *, tiles_per_seq):
    tn = INPROJ_TN
    tm = x_ref.shape[0]

    @pl.when(pl.program_id(0) % tiles_per_seq == 0)
    def _():
        tail_s[...] = jnp.zeros(tail_s.shape, F32)

    h = _rms_rows(x_ref[...], g_ref[...]).astype(BF16)
    for n0 in range(0, N_FEAT, tn):
        res = lax.dot_general(wf_ref[n0:n0 + tn, :], h, NT_DIMS, preferred_element_type=F32)
        if n0 < F_XS:
            res = res * _sigmoid(res)
        elif n0 < F_Q or n0 >= F_B:
            ch0 = n0 - F_XS if n0 < F_Q else n0 - F_B + D_SSM
            tail_rolled = [tail_s[k, ch0:ch0 + tn, :] for k in range(SSM_CONV - 1)]
            res, new_tail = _conv_silu_time_on_lanes(res, tail_rolled, cp_ref, ch0)
            for k in range(SSM_CONV - 1):
                tail_s[k, ch0:ch0 + tn, :] = new_tail[k]
        f_ref[n0:n0 + tn, :] = res.astype(BF16)
    k_ref[...] = jnp.dot(h, wk_ref[...], preferred_element_type=F32).astype(BF16)
    dt_ref[...] = lax.dot_general(wdt_ref[...], h, NT_DIMS, preferred_element_type=F32)


def _inproj(x2, ln_g, w_k, w_f_t, w_dt_t, conv_p, seq):
    tokens = x2.shape[0]
    tm = INPROJ_TM
    dt_rows = w_dt_t.shape[0]
    d_xbc = conv_p.shape[1]
    resident = dict(pipeline_mode=pl.Buffered(1))
    return pl.pallas_call(
        functools.partial(_inproj_kernel, tiles_per_seq=seq // tm),
        grid=(tokens // tm,),
        in_specs=[
            pl.BlockSpec((tm, D_MODEL), lambda i: (i, 0)),
            pl.BlockSpec((1, D_MODEL), lambda i: (0, 0)),
            pl.BlockSpec((D_MODEL, D_ATT), lambda i: (0, 0), **resident),
            pl.BlockSpec((N_FEAT, D_MODEL), lambda i: (0, 0), **resident),
            pl.BlockSpec((dt_rows, D_MODEL), lambda i: (0, 0), **resident),
            pl.BlockSpec((SSM_CONV + 1, d_xbc, LANES), lambda i: (0, 0, 0), **resident),
        ],
        out_specs=[
            pl.BlockSpec((tm, D_ATT), lambda i: (i, 0)),
            pl.BlockSpec((N_FEAT, tm), lambda i: (0, i)),
            pl.BlockSpec((dt_rows, tm), lambda i: (0, i)),
        ],
        out_shape=[
            jax.ShapeDtypeStruct((tokens, D_ATT), BF16),
            jax.ShapeDtypeStruct((N_FEAT, tokens), BF16),
            jax.ShapeDtypeStruct((dt_rows, tokens), F32),
        ],
        scratch_shapes=[pltpu.VMEM((SSM_CONV - 1, d_xbc, LANES), F32)],
        compiler_params=pltpu.CompilerParams(
            dimension_semantics=("arbitrary",), vmem_limit_bytes=52 * MIB),
        name="inproj",
    )(x2, ln_g, w_k, w_f_t, w_dt_t, conv_p)


def _alibi_slope(head):
    return float(2.0 ** (-8.0 * (head + 1) / N_ATT_HEADS))


def _attn_kernel(q_ref, k_ref, kown_ref, v_ref, vown_ref, g_ref, o_ref,
                 kmean_s, qm_s, sel_s, alibi_s, t_s, tmax_s, m_s, acc_s, out_s, *, nb):
    i = pl.program_id(1)
    blk = MOBA_BLOCK
    pair = 2 * HEAD_DIM

    @pl.when(i == 0)
    def _():
        kmean_s[...] = jnp.zeros(kmean_s.shape, F32)
        for j in range(nb):
            kb = k_ref[j * blk:(j + 1) * blk, :].astype(F32)
            kmean_s[j:j + 1, :] = jnp.mean(kb, axis=0, keepdims=True)
        key_off = lax.broadcasted_iota(jnp.int32, (blk, blk), 0).astype(F32)
        for head in range(N_ATT_HEADS):
            alibi_s[head] = _alibi_slope(head) * key_off

    cand = lax.broadcasted_iota(jnp.int32, (kmean_s.shape[0], blk), 0)
    valid = cand < i
    feat = lax.broadcasted_iota(jnp.int32, (pair, blk), 0)

    for hp in range(N_ATT_HEADS // 2):
        q2 = q_ref[hp * pair:(hp + 1) * pair, :].astype(F32) * (HEAD_DIM ** -0.5)
        for sub in range(2):
            head = 2 * hp + sub
            in_head = (feat < HEAD_DIM) if sub == 0 else (feat >= HEAD_DIM)
            qm_f = jnp.where(in_head, q2, 0.0)
            qm_s[head] = qm_f.astype(BF16)

            gate = jnp.dot(kmean_s[:, hp * pair:(hp + 1) * pair], qm_f,
                           precision=HIGHEST, preferred_element_type=F32)
            gate = jnp.where(valid, gate, NEG_INF)
            rank = jnp.zeros(gate.shape, F32)
            for jp in range(nb):
                gj = gate[jp:jp + 1, :]
                ge = jnp.where(gj >= gate, 1.0, 0.0)
                gt = jnp.where(gj > gate, 1.0, 0.0)
                rank = rank + jnp.where(cand > jp, ge, gt)
            sel_s[head] = jnp.where(valid, jnp.where(rank < MOBA_TOPK, 0.0, NEG_INF), NEG_INF)

    m_s[...] = jnp.full(m_s.shape, NEG_INF, F32)
    acc_s[...] = jnp.zeros(acc_s.shape, F32)

    causal = (lax.broadcasted_iota(jnp.int32, (blk, blk), 0)
              <= lax.broadcasted_iota(jnp.int32, (blk, blk), 1))

    def scores(tile, slot):
        for hp in range(N_ATT_HEADS // 2):
            cols = slice(hp * pair, (hp + 1) * pair)
            kj = kown_ref[:, cols] if tile is None else k_ref[tile * blk:(tile + 1) * blk, cols]
            for sub in range(2):
                head = 2 * hp + sub
                t = jnp.dot(kj, qm_s[head], preferred_element_type=F32) + alibi_s[head]
                if tile is None:
                    t = jnp.where(causal, t, NEG_INF)
                t_s[slot, head] = t
                quarter = blk // 4
                tq = jnp.maximum(jnp.maximum(t[0:quarter], t[quarter:2 * quarter]),
                                 jnp.maximum(t[2 * quarter:3 * quarter], t[3 * quarter:]))
                tmax_s[slot, head, 0:1, :] = jnp.max(tq, axis=0, keepdims=True)

    ones_rows = jnp.ones((BF16_SUBLANES, blk), BF16)

    def accumulate(tile, slot):
        for head in range(N_ATT_HEADS):
            rows = slice(head * HEAD_DIM, (head + 1) * HEAD_DIM)
            m_old = m_s[head, 0:1, :]
            if tile is None:
                m_new = jnp.maximum(m_old, tmax_s[slot, head, 0:1, :])
                shift = m_new
                v = vown_ref[rows, :]
            else:
                row = sel_s[head, tile:tile + 1, :] + (_alibi_slope(head) * blk) * (tile - i).astype(F32)
                m_new = jnp.maximum(m_old, tmax_s[slot, head, 0:1, :] + row)
                shift = m_new - row
                v = v_ref[rows, tile * blk:(tile + 1) * blk]
            p = jnp.exp(t_s[slot, head] - shift)
            alpha = jnp.exp(m_old - m_new)
            m_s[head, 0:1, :] = m_new
            vj = jnp.concatenate([v, ones_rows], axis=0)
            acc_s[head] = alpha * acc_s[head] + jnp.dot(vj, p.astype(BF16), preferred_element_type=F32)

    scores(None, 0)
    for j in range(nb - 1):
        @pl.when(j < i)
        def _(j=j):
            scores(j, (j + 1) % 2)
            accumulate(None if j == 0 else j - 1, j % 2)

    @pl.when(i == 0)
    def _():
        accumulate(None, 0)

    for j in range(nb - 1):
        @pl.when(i == j + 1)
        def _(j=j):
            accumulate(j, (j + 1) % 2)

    for head in range(N_ATT_HEADS):
        out_s[head * HEAD_DIM:(head + 1) * HEAD_DIM, :] = (
            acc_s[head, 0:HEAD_DIM, :] / acc_s[head, HEAD_DIM:HEAD_DIM + 1, :])
    att = out_s[...].T
    o_ref[...] = _rms_rows(att, g_ref[...]).astype(BF16)


def _attention(feat_t, k_rm, attn_g, batch, seq):
    nb = seq // MOBA_BLOCK
    cands = max(SUBLANES, nb)
    q_blk = F_Q // D_ATT
    v_blk = F_V // D_ATT
    return pl.pallas_call(
        functools.partial(_attn_kernel, nb=nb),
        grid=(batch, nb),
        in_specs=[
            pl.BlockSpec((D_ATT, MOBA_BLOCK), lambda b, i: (q_blk, b * nb + i)),
            pl.BlockSpec((seq, D_ATT), lambda b, i: (b, 0)),
            pl.BlockSpec((MOBA_BLOCK, D_ATT), lambda b, i: (b * nb + i, 0)),
            pl.BlockSpec((D_ATT, seq), lambda b, i: (v_blk, b)),
            pl.BlockSpec((D_ATT, MOBA_BLOCK), lambda b, i: (v_blk, b * nb + i)),
            pl.BlockSpec((1, D_ATT), lambda b, i: (0, 0)),
        ],
        out_specs=pl.BlockSpec((MOBA_BLOCK, D_ATT), lambda b, i: (b * nb + i, 0)),
        out_shape=jax.ShapeDtypeStruct((batch * seq, D_ATT), BF16),
        scratch_shapes=[
            pltpu.VMEM((cands, D_ATT), F32),
            pltpu.VMEM((N_ATT_HEADS, 2 * HEAD_DIM, MOBA_BLOCK), BF16),
            pltpu.VMEM((N_ATT_HEADS, cands, MOBA_BLOCK), F32),
            pltpu.VMEM((N_ATT_HEADS, MOBA_BLOCK, MOBA_BLOCK), F32),
            pltpu.VMEM((2, N_ATT_HEADS, MOBA_BLOCK, MOBA_BLOCK), F32),
            pltpu.VMEM((2, N_ATT_HEADS, SUBLANES, MOBA_BLOCK), F32),
            pltpu.VMEM((N_ATT_HEADS, SUBLANES, MOBA_BLOCK), F32),
            pltpu.VMEM((N_ATT_HEADS, HEAD_DIM + BF16_SUBLANES, MOBA_BLOCK), F32),
            pltpu.VMEM((D_ATT, MOBA_BLOCK), F32),
        ],
        compiler_params=pltpu.CompilerParams(
            dimension_semantics=("arbitrary", "arbitrary"), vmem_limit_bytes=32 * MIB),
        name="moba_attention",
    )(feat_t, k_rm, k_rm, feat_t, feat_t, attn_g)


P_DSKIP, P_NORMG = 0, 1
N_XS_PARAMS = 2


def _ssd_kernel(zs_ref, xs_ref, b_ref, c_ref, dt_ref, pxs_ref, pdt_ref, y_ref, state_s, *, seq):
    L = SSD_CHUNK
    r_i = lax.broadcasted_iota(jnp.int32, (L, L), 0)
    c_i = lax.broadcasted_iota(jnp.int32, (L, L), 1)
    upper = jnp.where(r_i <= c_i, 1.0, 0.0).astype(F32)
    lower = jnp.where(c_i <= r_i, 1.0, 0.0).astype(F32)
    ones = jnp.ones((L, L), F32)
    s_le_l = r_i <= c_i

    dt_bias = pdt_ref[0]
    a_neg = -jnp.exp(pdt_ref[1])

    state_s[...] = jnp.zeros(state_s.shape, F32)

    def chunk(c):
        off = c * L
        xs = xs_ref[:, pl.ds(off, L)].astype(F32)
        cm = c_ref[:, pl.ds(off, L)]

        raw = dt_ref[:, pl.ds(off, L)] + dt_bias
        dt = jnp.maximum(raw, 0.0) + jnp.log1p(jnp.exp(-jnp.abs(raw)))
        a_dt = dt * a_neg
        acs_row = jnp.dot(a_dt, upper, precision=HIGHEST, preferred_element_type=F32)
        acs_tot = jnp.dot(a_dt, ones, precision=HIGHEST, preferred_element_type=F32)
        acs_col = lax.dot_general(lower, a_dt, NT_DIMS, precision=HIGHEST,
                                  preferred_element_type=F32)

        b_t = b_ref[:, pl.ds(off, L)].astype(F32).T.astype(BF16)
        cb_t = jnp.dot(b_t, cm, preferred_element_type=F32)

        y_diag, x_state, out_decay, carry_decay = [], [], [], []
        for k in range(HEADS_PER_GROUP):
            rows = slice(k * HEAD_DIM, (k + 1) * HEAD_DIM)
            x_dt = xs[rows, :] * dt[k:k + 1, :]
            seg = acs_row[k:k + 1, :] - acs_col[:, k:k + 1]
            decay = jnp.exp(jnp.where(s_le_l, seg, NEG_INF))
            m_t = (cb_t * decay).astype(BF16)
            y_diag.append(jnp.dot(x_dt.astype(BF16), m_t, preferred_element_type=F32))
            to_end = jnp.exp(acs_tot[k:k + 1, :] - acs_row[k:k + 1, :])
            x_state.append((x_dt * to_end).astype(BF16))
            out_decay.append(jnp.broadcast_to(jnp.exp(acs_row[k:k + 1, :]), (HEAD_DIM, L)))
            carry_decay.append(jnp.broadcast_to(jnp.exp(acs_tot[k:k + 1, :]), (HEAD_DIM, L)))
        y_diag = jnp.concatenate(y_diag, axis=0)
        x_state = jnp.concatenate(x_state, axis=0)
        out_decay = jnp.concatenate(out_decay, axis=0)
        carry_decay = jnp.concatenate(carry_decay, axis=0)

        prev = state_s[...]
        y_off = jnp.dot(prev.astype(BF16), cm, preferred_element_type=F32) * out_decay
        state_s[...] = prev * carry_decay + jnp.dot(x_state, b_t, preferred_element_type=F32)

        y = y_diag + y_off + pxs_ref[P_DSKIP] * xs
        y = y * zs_ref[:, pl.ds(off, L)].astype(F32)
        ms = jnp.mean(y * y, axis=0, keepdims=True)
        yn = (y * lax.rsqrt(ms + EPS)) * pxs_ref[P_NORMG]
        y_ref[pl.ds(off, L), :] = yn.T.astype(BF16)

    for c in range(seq // L):
        chunk(c)


def _ssd(feat_t, dt_t, pxs, pdt, batch, seq):
    g_rows = GROUP_ROWS
    z_blk0 = F_Z // g_rows
    xs_blk0 = F_XS // g_rows
    b_blk0 = F_B // D_STATE
    c_blk0 = F_C // D_STATE
    return pl.pallas_call(
        functools.partial(_ssd_kernel, seq=seq),
        grid=(batch, N_SSM_GROUPS),
        in_specs=[
            pl.BlockSpec((g_rows, seq), lambda b, g: (z_blk0 + g, b)),
            pl.BlockSpec((g_rows, seq), lambda b, g: (xs_blk0 + g, b)),
            pl.BlockSpec((D_STATE, seq), lambda b, g: (b_blk0 + g, b)),
            pl.BlockSpec((D_STATE, seq), lambda b, g: (c_blk0 + g, b)),
            pl.BlockSpec((DT_ROWS_PER_GROUP, seq), lambda b, g: (g, b)),
            pl.BlockSpec((N_XS_PARAMS, g_rows, LANES), lambda b, g: (0, g, 0)),
            pl.BlockSpec((2, DT_ROWS_PER_GROUP, LANES), lambda b, g: (0, g, 0)),
        ],
        out_specs=pl.BlockSpec((seq, g_rows), lambda b, g: (b, g)),
        out_shape=jax.ShapeDtypeStruct((batch * seq, D_SSM), BF16),
        scratch_shapes=[pltpu.VMEM((g_rows, D_STATE), F32)],
        compiler_params=pltpu.CompilerParams(
            dimension_semantics=("arbitrary", "arbitrary"), vmem_limit_bytes=32 * MIB),
        name="ssd_mixer",
    )(feat_t, feat_t, feat_t, feat_t, dt_t, pxs, pdt)


OUTPROJ_TM = 1024


def _outproj_kernel(x_ref, a_ref, y_ref, wa_ref, wy_ref, o_ref):
    o_ref[...] = (x_ref[...]
                  + jnp.dot(a_ref[...], wa_ref[...], preferred_element_type=F32)
                  + jnp.dot(y_ref[...], wy_ref[...], preferred_element_type=F32))


def _outproj(x2, att, y, w_att, w_ssm):
    tokens = x2.shape[0]
    tm = OUTPROJ_TM
    resident = dict(pipeline_mode=pl.Buffered(1))
    return pl.pallas_call(
        _outproj_kernel,
        grid=(tokens // tm,),
        in_specs=[
            pl.BlockSpec((tm, D_MODEL), lambda i: (i, 0)),
            pl.BlockSpec((tm, D_ATT), lambda i: (i, 0)),
            pl.BlockSpec((tm, D_SSM), lambda i: (i, 0)),
            pl.BlockSpec((D_ATT, D_MODEL), lambda i: (0, 0), **resident),
            pl.BlockSpec((D_SSM, D_MODEL), lambda i: (0, 0), **resident),
        ],
        out_specs=pl.BlockSpec((tm, D_MODEL), lambda i: (i, 0)),
        out_shape=jax.ShapeDtypeStruct((tokens, D_MODEL), F32),
        compiler_params=pltpu.CompilerParams(
            dimension_semantics=("arbitrary",), vmem_limit_bytes=40 * MIB),
        name="outproj",
    )(x2, att, y, w_att, w_ssm)


FFN_TM = 512
FFN_TN = 256
FFN_HALO = BF16_SUBLANES
FFN_U_BUFFERS = 4


def _ffn_kernel(x_ref, g2_ref, wup_ref, cw_ref, cb_ref, wdn_ref, gf_ref, o_ref,
                h_s, u_s, act_s, *, tiles_per_seq):
    i = pl.program_id(0)
    tm = FFN_TM
    halo = FFN_HALO

    @pl.when(i % tiles_per_seq == 0)
    def _():
        h_s[0:halo, :] = jnp.zeros((halo, D_MODEL), BF16)

    x = x_ref[...]
    h_s[halo:, :] = _rms_rows(x, g2_ref[...]).astype(BF16)
    h_ext = h_s[...]

    def conv(col0, width, buf):
        u_s[buf] = jnp.dot(h_ext, wup_ref[:, col0:col0 + width], preferred_element_type=F32)
        acc = cb_ref[:, col0:col0 + width]
        for k in range(FFN_CONV):
            shift = FFN_CONV - 1 - k
            acc = acc + cw_ref[k:k + 1, col0:col0 + width] * u_s[buf, halo - shift:halo - shift + tm, :]
        return acc

    for c, n0 in enumerate(range(0, D_FF, FFN_TN)):
        gate = conv(n0, FFN_TN, (2 * c) % FFN_U_BUFFERS)
        val = conv(D_FF + n0, FFN_TN, (2 * c + 1) % FFN_U_BUFFERS)
        act_s[:, n0:n0 + FFN_TN] = ((gate * _sigmoid(gate)) * val).astype(BF16)

    h_s[0:halo, :] = h_s[tm:tm + halo, :]
    half = (D_FF // (2 * LANES)) * LANES
    x2 = (x + jnp.dot(act_s[:, :half], wdn_ref[:half, :], preferred_element_type=F32)
          + jnp.dot(act_s[:, half:], wdn_ref[half:, :], preferred_element_type=F32))
    o_ref[...] = _rms_rows(x2, gf_ref[...])


def _ffn(x1, ln2_g, w_up, conv_w, conv_b, w_down, lnf_g, seq):
    tokens = x1.shape[0]
    tm = FFN_TM
    resident = dict(pipeline_mode=pl.Buffered(1))
    return pl.pallas_call(
        functools.partial(_ffn_kernel, tiles_per_seq=seq // tm),
        grid=(tokens // tm,),
        in_specs=[
            pl.BlockSpec((tm, D_MODEL), lambda i: (i, 0)),
            pl.BlockSpec((1, D_MODEL), lambda i: (0, 0)),
            pl.BlockSpec((D_MODEL, 2 * D_FF), lambda i: (0, 0), **resident),
            pl.BlockSpec((FFN_CONV, 2 * D_FF), lambda i: (0, 0)),
            pl.BlockSpec((1, 2 * D_FF), lambda i: (0, 0)),
            pl.BlockSpec((D_FF, D_MODEL), lambda i: (0, 0), **resident),
            pl.BlockSpec((1, D_MODEL), lambda i: (0, 0)),
        ],
        out_specs=pl.BlockSpec((tm, D_MODEL), lambda i: (i, 0)),
        out_shape=jax.ShapeDtypeStruct((tokens, D_MODEL), F32),
        scratch_shapes=[
            pltpu.VMEM((FFN_HALO + tm, D_MODEL), BF16),
            pltpu.VMEM((FFN_U_BUFFERS, FFN_HALO + tm, FFN_TN), F32),
            pltpu.VMEM((tm, D_FF), BF16),
        ],
        compiler_params=pltpu.CompilerParams(
            dimension_semantics=("arbitrary",), vmem_limit_bytes=48 * MIB),
        name="conv_ffn",
    )(x1, ln2_g, w_up, conv_w, conv_b, w_down, lnf_g)


def _lane_bcast(v):
    return jnp.broadcast_to(v[..., None], v.shape + (LANES,)).astype(F32)


def _group_pad(v):
    v = v.reshape(N_SSM_GROUPS, HEADS_PER_GROUP)
    v = jnp.pad(v, ((0, 0), (0, DT_ROWS_PER_GROUP - HEADS_PER_GROUP)))
    return v.reshape(N_SSM_GROUPS * DT_ROWS_PER_GROUP)


def _layer(x2, batch, seq, ln1_g, w_in, ssm_conv_w, ssm_conv_b, dt_bias, a_log, d_skip,
           attn_norm_g, ssm_norm_g, w_out, ln2_g, w_up, ffn_conv_w, ffn_conv_b, w_down, out_g):
    c_q, c_k, c_v, c_z = 0, D_ATT, 2 * D_ATT, 3 * D_ATT
    c_xs = c_z + D_SSM
    c_b = c_xs + D_SSM
    c_c = c_b + N_SSM_GROUPS * D_STATE
    c_dt = c_c + N_SSM_GROUPS * D_STATE

    w_k = w_in[:, c_k:c_v].astype(BF16)
    w_f_t = jnp.concatenate(
        [w_in[:, c_z:c_xs], w_in[:, c_xs:c_b], w_in[:, c_q:c_k], w_in[:, c_v:c_z],
         w_in[:, c_b:c_c], w_in[:, c_c:c_dt]], axis=1).T.astype(BF16)
    w_dt_t = w_in[:, c_dt:].T.reshape(N_SSM_GROUPS, HEADS_PER_GROUP, D_MODEL)
    w_dt_t = jnp.pad(w_dt_t, ((0, 0), (0, DT_ROWS_PER_GROUP - HEADS_PER_GROUP), (0, 0)))
    w_dt_t = w_dt_t.reshape(N_SSM_GROUPS * DT_ROWS_PER_GROUP, D_MODEL).astype(BF16)

    conv_p = _lane_bcast(jnp.concatenate([ssm_conv_w, ssm_conv_b[None, :]], axis=0))
    k_rm, feat_t, dt_t = _inproj(x2, ln1_g[None, :], w_k, w_f_t, w_dt_t, conv_p, seq)

    att = _attention(feat_t, k_rm, attn_norm_g[None, :], batch, seq)

    pxs = _lane_bcast(jnp.stack([jnp.repeat(d_skip, HEAD_DIM), ssm_norm_g], axis=0))
    pdt = _lane_bcast(jnp.stack([_group_pad(dt_bias), _group_pad(a_log)], axis=0))
    y = _ssd(feat_t, dt_t, pxs, pdt, batch, seq)

    w_out_b = w_out.astype(BF16)
    x1 = _outproj(x2, att, y, w_out_b[:D_ATT], w_out_b[D_ATT:])

    return _ffn(x1, ln2_g[None, :], w_up.astype(BF16), ffn_conv_w, ffn_conv_b[None, :],
                w_down.astype(BF16), out_g[None, :], seq)


def kernel(x, ln1_g, w_in, ssm_conv_w, ssm_conv_b, dt_bias, a_log, d_skip, attn_norm_g,
           ssm_norm_g, w_out, ln2_g, w_up, ffn_conv_w, ffn_conv_b, w_down, lnf_g):
    batch, seq, d_model = x.shape
    depth = ln1_g.shape[0]
    assert d_model == D_MODEL and depth == 1
    assert seq % MOBA_BLOCK == 0 and seq % FFN_TM == 0 and seq % INPROJ_TM == 0
    assert (batch * seq) % OUTPROJ_TM == 0
    x2 = x.reshape(batch * seq, d_model)
    out = _layer(x2, batch, seq, ln1_g[0], w_in[0], ssm_conv_w[0], ssm_conv_b[0], dt_bias[0],
                 a_log[0], d_skip[0], attn_norm_g[0], ssm_norm_g[0], w_out[0], ln2_g[0],
                 w_up[0], ffn_conv_w[0], ffn_conv_b[0], w_down[0], lnf_g)
    return out.reshape(batch, seq, d_model)
```

```python
import functools

import jax
import jax.numpy as jnp
from jax import lax
from jax.experimental import pallas as pl
from jax.experimental.pallas import tpu as pltpu

F32 = jnp.float32
BF16 = jnp.bfloat16
HIGHEST = lax.Precision.HIGHEST

EPS = 1e-6
NEG_INF = -1e30
LOG2_E = 1.4426950408889634

D_MODEL = 1024
HEAD_DIM = 64
D_ATT = 512
N_ATT_HEADS = D_ATT // HEAD_DIM
MOBA_BLOCK = 256
MOBA_TOPK = 3
D_SSM = 1536
N_SSM_GROUPS = 4
HEADS_PER_GROUP = 6
GROUP_ROWS = HEADS_PER_GROUP * HEAD_DIM
D_STATE = 128
SSM_CONV = 4
SSD_CHUNK = 128
D_FF = 2816
FFN_CONV = 3

LANES = 128
SUBLANES = 8
BF16_SUBLANES = 16
DT_ROWS_PER_GROUP = SUBLANES

F_Z = 0
F_XS = F_Z + D_SSM
F_Q = F_XS + D_SSM
F_V = F_Q + D_ATT
F_B = F_V + D_ATT
F_C = F_B + N_SSM_GROUPS * D_STATE
N_FEAT = F_C + N_SSM_GROUPS * D_STATE

NT_DIMS = (((1,), (1,)), ((), ()))

MIB = 1024 * 1024


def _sigmoid(x):
    return 1.0 / (1.0 + jnp.exp(-x))


def _rms_rows(x, gain):
    ms = jnp.mean(x * x, axis=-1, keepdims=True)
    return (x * lax.rsqrt(ms + EPS)) * gain


INPROJ_TM = 512
INPROJ_TN = 256


def _conv_silu_time_on_lanes(cur, tail_rolled, cp_ref, ch0):
    rows, width = cur.shape
    lane = lax.broadcasted_iota(jnp.int32, (1, LANES), 1)
    taps = [cp_ref[k, ch0:ch0 + rows, :] for k in range(SSM_CONV)]
    bias = cp_ref[SSM_CONV, ch0:ch0 + rows, :]
    blocks = [cur[:, c * LANES:(c + 1) * LANES] for c in range(width // LANES)]
    rolled = [tail_rolled] + [[pltpu.roll(b, k, axis=1) for k in range(1, SSM_CONV)] for b in blocks]
    out = []
    for c, block in enumerate(blocks):
        acc = bias + taps[SSM_CONV - 1] * block
        for k in range(1, SSM_CONV):
            shifted = jnp.where(lane >= k, rolled[c + 1][k - 1], rolled[c][k - 1])
            acc = acc + taps[SSM_CONV - 1 - k] * shifted
        out.append(acc * _sigmoid(acc))
    return jnp.concatenate(out, axis=1), rolled[-1]


def _inproj_kernel(x_ref, g_ref, wk_ref, wf_ref, wdt_ref, cp_ref, k_ref, f_ref, dt_ref, tail_s,
                   *, tiles_per_seq):
    tn = INPROJ_TN
    tm = x_ref.shape[0]

    @pl.when(pl.program_id(0) % tiles_per_seq == 0)
    def _():
        tail_s[...] = jnp.zeros(tail_s.shape, F32)

    h = _rms_rows(x_ref[...], g_ref[...]).astype(BF16)
    for n0 in range(0, N_FEAT, tn):
        res = lax.dot_general(wf_ref[n0:n0 + tn, :], h, NT_DIMS, preferred_element_type=F32)
        if n0 < F_XS:
            res = res * _sigmoid(res)
        elif n0 < F_Q or n0 >= F_B:
            ch0 = n0 - F_XS if n0 < F_Q else n0 - F_B + D_SSM
            tail_rolled = [tail_s[k, ch0:ch0 + tn, :] for k in range(SSM_CONV - 1)]
            res, new_tail = _conv_silu_time_on_lanes(res, tail_rolled, cp_ref, ch0)
            for k in range(SSM_CONV - 1):
                tail_s[k, ch0:ch0 + tn, :] = new_tail[k]
        f_ref[n0:n0 + tn, :] = res.astype(BF16)
    k_ref[...] = jnp.dot(h, wk_ref[...], preferred_element_type=F32).astype(BF16)
    dt_ref[...] = lax.dot_general(wdt_ref[...], h, NT_DIMS, preferred_element_type=F32)


def _inproj(x2, ln_g, w_k, w_f_t, w_dt_t, conv_p, seq):
    tokens = x2.shape[0]
    tm = INPROJ_TM
    dt_rows = w_dt_t.shape[0]
    d_xbc = conv_p.shape[1]
    resident = dict(pipeline_mode=pl.Buffered(1))
    return pl.pallas_call(
        functools.partial(_inproj_kernel, tiles_per_seq=seq // tm),
        grid=(tokens // tm,),
        in_specs=[
            pl.BlockSpec((tm, D_MODEL), lambda i: (i, 0)),
            pl.BlockSpec((1, D_MODEL), lambda i: (0, 0)),
            pl.BlockSpec((D_MODEL, D_ATT), lambda i: (0, 0), **resident),
            pl.BlockSpec((N_FEAT, D_MODEL), lambda i: (0, 0), **resident),
            pl.BlockSpec((dt_rows, D_MODEL), lambda i: (0, 0), **resident),
            pl.BlockSpec((SSM_CONV + 1, d_xbc, LANES), lambda i: (0, 0, 0), **resident),
        ],
        out_specs=[
            pl.BlockSpec((tm, D_ATT), lambda i: (i, 0)),
            pl.BlockSpec((N_FEAT, tm), lambda i: (0, i)),
            pl.BlockSpec((dt_rows, tm), lambda i: (0, i)),
        ],
        out_shape=[
            jax.ShapeDtypeStruct((tokens, D_ATT), BF16),
            jax.ShapeDtypeStruct((N_FEAT, tokens), BF16),
            jax.ShapeDtypeStruct((dt_rows, tokens), F32),
        ],
        scratch_shapes=[pltpu.VMEM((SSM_CONV - 1, d_xbc, LANES), F32)],
        compiler_params=pltpu.CompilerParams(
            dimension_semantics=("arbitrary",), vmem_limit_bytes=52 * MIB),
        name="inproj",
    )(x2, ln_g, w_k, w_f_t, w_dt_t, conv_p)


def _alibi_slope(head):
    return float(2.0 ** (-8.0 * (head + 1) / N_ATT_HEADS))


def _attn_kernel(q_ref, k_ref, kown_ref, v_ref, vown_ref, g_ref, o_ref,
                 kmean_s, qm_s, sel_s, alibi_s, t_s, tmax_s, m_s, acc_s, out_s, *, nb):
    i = pl.program_id(1)
    blk = MOBA_BLOCK
    pair = 2 * HEAD_DIM

    @pl.when(i == 0)
    def _():
        kmean_s[...] = jnp.zeros(kmean_s.shape, F32)
        for j in range(nb):
            kb = k_ref[j * blk:(j + 1) * blk, :].astype(F32)
            kmean_s[j:j + 1, :] = jnp.mean(kb, axis=0, keepdims=True)
        key_off = lax.broadcasted_iota(jnp.int32, (blk, blk), 0).astype(F32)
        for head in range(N_ATT_HEADS):
            alibi_s[head] = (_alibi_slope(head) * LOG2_E) * key_off

    cand = lax.broadcasted_iota(jnp.int32, (kmean_s.shape[0], blk), 0)
    valid = cand < i
    feat = lax.broadcasted_iota(jnp.int32, (pair, blk), 0)

    for hp in range(N_ATT_HEADS // 2):
        q2 = q_ref[hp * pair:(hp + 1) * pair, :].astype(F32) * (HEAD_DIM ** -0.5)
        for sub in range(2):
            head = 2 * hp + sub
            in_head = (feat < HEAD_DIM) if sub == 0 else (feat >= HEAD_DIM)
            qm_f = jnp.where(in_head, q2, 0.0)
            qm_s[head] = (qm_f * LOG2_E).astype(BF16)

            gate = jnp.dot(kmean_s[:, hp * pair:(hp + 1) * pair], qm_f,
                           precision=HIGHEST, preferred_element_type=F32)
            gate = jnp.where(valid, gate, NEG_INF)
            rank = jnp.zeros(gate.shape, F32)
            for jp in range(nb):
                gj = gate[jp:jp + 1, :]
                ge = jnp.where(gj >= gate, 1.0, 0.0)
                gt = jnp.where(gj > gate, 1.0, 0.0)
                rank = rank + jnp.where(cand > jp, ge, gt)
            sel_s[head] = jnp.where(valid, jnp.where(rank < MOBA_TOPK, 0.0, NEG_INF), NEG_INF)

    m_s[...] = jnp.full(m_s.shape, NEG_INF, F32)
    acc_s[...] = jnp.zeros(acc_s.shape, F32)

    causal = (lax.broadcasted_iota(jnp.int32, (blk, blk), 0)
              <= lax.broadcasted_iota(jnp.int32, (blk, blk), 1))

    def scores(tile, slot):
        for hp in range(N_ATT_HEADS // 2):
            cols = slice(hp * pair, (hp + 1) * pair)
            kj = kown_ref[:, cols] if tile is None else k_ref[tile * blk:(tile + 1) * blk, cols]
            for sub in range(2):
                head = 2 * hp + sub
                t = jnp.dot(kj, qm_s[head], preferred_element_type=F32) + alibi_s[head]
                if tile is None:
                    t = jnp.where(causal, t, NEG_INF)
                t_s[slot, head] = t
                quarter = blk // 4
                tq = jnp.maximum(jnp.maximum(t[0:quarter], t[quarter:2 * quarter]),
                                 jnp.maximum(t[2 * quarter:3 * quarter], t[3 * quarter:]))
                tmax_s[slot, head, 0:1, :] = jnp.max(tq, axis=0, keepdims=True)

    ones_rows = jnp.ones((BF16_SUBLANES, blk), BF16)

    def accumulate(tile, slot):
        for head in range(N_ATT_HEADS):
            rows = slice(head * HEAD_DIM, (head + 1) * HEAD_DIM)
            m_old = m_s[head, 0:1, :]
            if tile is None:
                m_new = jnp.maximum(m_old, tmax_s[slot, head, 0:1, :])
                shift = m_new
                v = vown_ref[rows, :]
            else:
                row = (sel_s[head, tile:tile + 1, :]
                       + (_alibi_slope(head) * LOG2_E * blk) * (tile - i).astype(F32))
                m_new = jnp.maximum(m_old, tmax_s[slot, head, 0:1, :] + row)
                shift = m_new - row
                v = v_ref[rows, tile * blk:(tile + 1) * blk]
            p = jnp.exp2(t_s[slot, head] - shift)
            alpha = jnp.exp2(m_old - m_new)
            m_s[head, 0:1, :] = m_new
            vj = jnp.concatenate([v, ones_rows], axis=0)
            acc_s[head] = alpha * acc_s[head] + jnp.dot(vj, p.astype(BF16), preferred_element_type=F32)

    scores(None, 0)
    for j in range(nb - 1):
        @pl.when(j < i)
        def _(j=j):
            scores(j, (j + 1) % 2)
            accumulate(None if j == 0 else j - 1, j % 2)

    @pl.when(i == 0)
    def _():
        accumulate(None, 0)

    for j in range(nb - 1):
        @pl.when(i == j + 1)
        def _(j=j):
            accumulate(j, (j + 1) % 2)

    for head in range(N_ATT_HEADS):
        out_s[head * HEAD_DIM:(head + 1) * HEAD_DIM, :] = (
            acc_s[head, 0:HEAD_DIM, :] / acc_s[head, HEAD_DIM:HEAD_DIM + 1, :])
    att = out_s[...].T
    o_ref[...] = _rms_rows(att, g_ref[...]).astype(BF16)


def _attention(feat_t, k_rm, attn_g, batch, seq):
    nb = seq // MOBA_BLOCK
    cands = max(SUBLANES, nb)
    q_blk = F_Q // D_ATT
    v_blk = F_V // D_ATT
    return pl.pallas_call(
        functools.partial(_attn_kernel, nb=nb),
        grid=(batch, nb),
        in_specs=[
            pl.BlockSpec((D_ATT, MOBA_BLOCK), lambda b, i: (q_blk, b * nb + i)),
            pl.BlockSpec((seq, D_ATT), lambda b, i: (b, 0)),
            pl.BlockSpec((MOBA_BLOCK, D_ATT), lambda b, i: (b * nb + i, 0)),
            pl.BlockSpec((D_ATT, seq), lambda b, i: (v_blk, b)),
            pl.BlockSpec((D_ATT, MOBA_BLOCK), lambda b, i: (v_blk, b * nb + i)),
            pl.BlockSpec((1, D_ATT), lambda b, i: (0, 0)),
        ],
        out_specs=pl.BlockSpec((MOBA_BLOCK, D_ATT), lambda b, i: (b * nb + i, 0)),
        out_shape=jax.ShapeDtypeStruct((batch * seq, D_ATT), BF16),
        scratch_shapes=[
            pltpu.VMEM((cands, D_ATT), F32),
            pltpu.VMEM((N_ATT_HEADS, 2 * HEAD_DIM, MOBA_BLOCK), BF16),
            pltpu.VMEM((N_ATT_HEADS, cands, MOBA_BLOCK), F32),
            pltpu.VMEM((N_ATT_HEADS, MOBA_BLOCK, MOBA_BLOCK), F32),
            pltpu.VMEM((2, N_ATT_HEADS, MOBA_BLOCK, MOBA_BLOCK), F32),
            pltpu.VMEM((2, N_ATT_HEADS, SUBLANES, MOBA_BLOCK), F32),
            pltpu.VMEM((N_ATT_HEADS, SUBLANES, MOBA_BLOCK), F32),
            pltpu.VMEM((N_ATT_HEADS, HEAD_DIM + BF16_SUBLANES, MOBA_BLOCK), F32),
            pltpu.VMEM((D_ATT, MOBA_BLOCK), F32),
        ],
        compiler_params=pltpu.CompilerParams(
            dimension_semantics=("arbitrary", "arbitrary"), vmem_limit_bytes=32 * MIB),
        name="moba_attention",
    )(feat_t, k_rm, k_rm, feat_t, feat_t, attn_g)


P_DSKIP, P_NORMG = 0, 1
N_XS_PARAMS = 2


def _ssd_kernel(zs_ref, xs_ref, b_ref, c_ref, dt_ref, pxs_ref, pdt_ref, y_ref,
                bt_s, cbt_s, ylocal_s, newst_s, state_s, *, seq):
    L = SSD_CHUNK
    n_chunks = seq // L
    r_i = lax.broadcasted_iota(jnp.int32, (L, L), 0)
    c_i = lax.broadcasted_iota(jnp.int32, (L, L), 1)
    upper = jnp.where(r_i <= c_i, 1.0, 0.0).astype(F32)
    s_le_l = r_i <= c_i

    dt_bias = pdt_ref[0]
    a_neg = -jnp.exp(pdt_ref[1])

    raw = jnp.concatenate([dt_ref[:, c * L:(c + 1) * L] + dt_bias for c in range(n_chunks)], axis=0)
    dt_all = jnp.maximum(raw, 0.0) + jnp.log1p(jnp.exp(-jnp.abs(raw)))
    a_dt = dt_all * jnp.concatenate([a_neg] * n_chunks, axis=0)
    acs_all = jnp.dot(a_dt, upper, precision=HIGHEST, preferred_element_type=F32)
    acs_log2 = acs_all * LOG2_E
    acs_log2_t = acs_log2.T
    acs_end = jnp.broadcast_to(acs_all[:, L - 1:L], acs_all.shape)
    to_end_all = jnp.exp(acs_end - acs_all)
    out_decay_all = jnp.exp(acs_all)
    carry_decay_all = jnp.exp(acs_end)

    def rows_of(c):
        return slice(c * DT_ROWS_PER_GROUP, (c + 1) * DT_ROWS_PER_GROUP)

    def head_rows(k):
        return slice(k * HEAD_DIM, (k + 1) * HEAD_DIM)

    for c in range(n_chunks):
        b_t = b_ref[:, c * L:(c + 1) * L].astype(F32).T.astype(BF16)
        bt_s[c] = b_t
        cbt_s[c] = jnp.dot(b_t, c_ref[:, c * L:(c + 1) * L], preferred_element_type=F32)

    for c in range(n_chunks):
        cols = slice(c * L, (c + 1) * L)
        xs = xs_ref[:, cols].astype(F32)
        dt = dt_all[rows_of(c)]
        acs_row = acs_log2[rows_of(c)]
        acs_col = acs_log2_t[:, rows_of(c)]
        to_end = to_end_all[rows_of(c)]
        cb_t = cbt_s[c]
        y_diag, x_state = [], []
        for k in range(HEADS_PER_GROUP):
            x_dt = xs[head_rows(k), :] * dt[k:k + 1, :]
            seg = acs_row[k:k + 1, :] - acs_col[:, k:k + 1]
            decay = jnp.exp2(jnp.where(s_le_l, seg, NEG_INF))
            m_t = (cb_t * decay).astype(BF16)
            y_diag.append(jnp.dot(x_dt.astype(BF16), m_t, preferred_element_type=F32))
            x_state.append((x_dt * to_end[k:k + 1, :]).astype(BF16))
        ylocal_s[:, cols] = jnp.concatenate(y_diag, axis=0) + pxs_ref[P_DSKIP] * xs
        newst_s[c] = jnp.dot(jnp.concatenate(x_state, axis=0), bt_s[c], preferred_element_type=F32)

    state_s[...] = jnp.zeros(state_s.shape, F32)
    for c in range(n_chunks):
        cols = slice(c * L, (c + 1) * L)
        out_decay = jnp.concatenate(
            [jnp.broadcast_to(out_decay_all[rows_of(c)][k:k + 1, :], (HEAD_DIM, L))
             for k in range(HEADS_PER_GROUP)], axis=0)
        carry_decay = jnp.concatenate(
            [jnp.broadcast_to(carry_decay_all[rows_of(c)][k:k + 1, :], (HEAD_DIM, L))
             for k in range(HEADS_PER_GROUP)], axis=0)
        prev = state_s[...]
        y_off = jnp.dot(prev.astype(BF16), c_ref[:, cols], preferred_element_type=F32) * out_decay
        state_s[...] = prev * carry_decay + newst_s[c]
        y = (ylocal_s[:, cols] + y_off) * zs_ref[:, cols].astype(F32)
        ms = jnp.mean(y * y, axis=0, keepdims=True)
        yn = (y * lax.rsqrt(ms + EPS)) * pxs_ref[P_NORMG]
        y_ref[cols, :] = yn.T.astype(BF16)


def _ssd(feat_t, dt_t, pxs, pdt, batch, seq):
    g_rows = GROUP_ROWS
    z_blk0 = F_Z // g_rows
    xs_blk0 = F_XS // g_rows
    b_blk0 = F_B // D_STATE
    c_blk0 = F_C // D_STATE
    return pl.pallas_call(
        functools.partial(_ssd_kernel, seq=seq),
        grid=(batch, N_SSM_GROUPS),
        in_specs=[
            pl.BlockSpec((g_rows, seq), lambda b, g: (z_blk0 + g, b)),
            pl.BlockSpec((g_rows, seq), lambda b, g: (xs_blk0 + g, b)),
            pl.BlockSpec((D_STATE, seq), lambda b, g: (b_blk0 + g, b)),
            pl.BlockSpec((D_STATE, seq), lambda b, g: (c_blk0 + g, b)),
            pl.BlockSpec((DT_ROWS_PER_GROUP, seq), lambda b, g: (g, b)),
            pl.BlockSpec((N_XS_PARAMS, g_rows, LANES), lambda b, g: (0, g, 0)),
            pl.BlockSpec((2, DT_ROWS_PER_GROUP, LANES), lambda b, g: (0, g, 0)),
        ],
        out_specs=pl.BlockSpec((seq, g_rows), lambda b, g: (b, g)),
        out_shape=jax.ShapeDtypeStruct((batch * seq, D_SSM), BF16),
        scratch_shapes=[
            pltpu.VMEM((seq // SSD_CHUNK, SSD_CHUNK, D_STATE), BF16),
            pltpu.VMEM((seq // SSD_CHUNK, SSD_CHUNK, SSD_CHUNK), F32),
            pltpu.VMEM((g_rows, seq), F32),
            pltpu.VMEM((seq // SSD_CHUNK, g_rows, D_STATE), F32),
            pltpu.VMEM((g_rows, D_STATE), F32),
        ],
        compiler_params=pltpu.CompilerParams(
            dimension_semantics=("arbitrary", "arbitrary"), vmem_limit_bytes=40 * MIB),
        name="ssd_mixer",
    )(feat_t, feat_t, feat_t, feat_t, dt_t, pxs, pdt)


OUTPROJ_TM = 1024


def _outproj_kernel(x_ref, a_ref, y_ref, wa_ref, wy_ref, o_ref):
    o_ref[...] = (x_ref[...]
                  + jnp.dot(a_ref[...], wa_ref[...], preferred_element_type=F32)
                  + jnp.dot(y_ref[...], wy_ref[...], preferred_element_type=F32))


def _outproj(x2, att, y, w_att, w_ssm):
    tokens = x2.shape[0]
    tm = OUTPROJ_TM
    resident = dict(pipeline_mode=pl.Buffered(1))
    return pl.pallas_call(
        _outproj_kernel,
        grid=(tokens // tm,),
        in_specs=[
            pl.BlockSpec((tm, D_MODEL), lambda i: (i, 0)),
            pl.BlockSpec((tm, D_ATT), lambda i: (i, 0)),
            pl.BlockSpec((tm, D_SSM), lambda i: (i, 0)),
            pl.BlockSpec((D_ATT, D_MODEL), lambda i: (0, 0), **resident),
            pl.BlockSpec((D_SSM, D_MODEL), lambda i: (0, 0), **resident),
        ],
        out_specs=pl.BlockSpec((tm, D_MODEL), lambda i: (i, 0)),
        out_shape=jax.ShapeDtypeStruct((tokens, D_MODEL), F32),
        compiler_params=pltpu.CompilerParams(
            dimension_semantics=("arbitrary",), vmem_limit_bytes=40 * MIB),
        name="outproj",
    )(x2, att, y, w_att, w_ssm)


FFN_TM = 512
FFN_TN = 256
FFN_HALO = BF16_SUBLANES
FFN_U_BUFFERS = 4


def _ffn_kernel(x_ref, g2_ref, wup_ref, cw_ref, cb_ref, wdn_ref, gf_ref, o_ref,
                h_s, u_s, act_s, *, tiles_per_seq):
    i = pl.program_id(0)
    tm = FFN_TM
    halo = FFN_HALO

    @pl.when(i % tiles_per_seq == 0)
    def _():
        h_s[0:halo, :] = jnp.zeros((halo, D_MODEL), BF16)

    x = x_ref[...]
    h_s[halo:, :] = _rms_rows(x, g2_ref[...]).astype(BF16)
    h_ext = h_s[...]

    def conv(col0, width, buf):
        u_s[buf] = jnp.dot(h_ext, wup_ref[:, col0:col0 + width], preferred_element_type=F32)
        acc = cb_ref[:, col0:col0 + width]
        for k in range(FFN_CONV):
            shift = FFN_CONV - 1 - k
            acc = acc + cw_ref[k:k + 1, col0:col0 + width] * u_s[buf, halo - shift:halo - shift + tm, :]
        return acc

    for c, n0 in enumerate(range(0, D_FF, FFN_TN)):
        gate = conv(n0, FFN_TN, (2 * c) % FFN_U_BUFFERS)
        val = conv(D_FF + n0, FFN_TN, (2 * c + 1) % FFN_U_BUFFERS)
        act_s[:, n0:n0 + FFN_TN] = ((gate * _sigmoid(gate)) * val).astype(BF16)

    h_s[0:halo, :] = h_s[tm:tm + halo, :]
    half = (D_FF // (2 * LANES)) * LANES
    x2 = (x + jnp.dot(act_s[:, :half], wdn_ref[:half, :], preferred_element_type=F32)
          + jnp.dot(act_s[:, half:], wdn_ref[half:, :], preferred_element_type=F32))
    o_ref[...] = _rms_rows(x2, gf_ref[...])


def _ffn(x1, ln2_g, w_up, conv_w, conv_b, w_down, lnf_g, seq):
    tokens = x1.shape[0]
    tm = FFN_TM
    resident = dict(pipeline_mode=pl.Buffered(1))
    return pl.pallas_call(
        functools.partial(_ffn_kernel, tiles_per_seq=seq // tm),
        grid=(tokens // tm,),
        in_specs=[
            pl.BlockSpec((tm, D_MODEL), lambda i: (i, 0)),
            pl.BlockSpec((1, D_MODEL), lambda i: (0, 0)),
            pl.BlockSpec((D_MODEL, 2 * D_FF), lambda i: (0, 0), **resident),
            pl.BlockSpec((FFN_CONV, 2 * D_FF), lambda i: (0, 0)),
            pl.BlockSpec((1, 2 * D_FF), lambda i: (0, 0)),
            pl.BlockSpec((D_FF, D_MODEL), lambda i: (0, 0), **resident),
            pl.BlockSpec((1, D_MODEL), lambda i: (0, 0)),
        ],
        out_specs=pl.BlockSpec((tm, D_MODEL), lambda i: (i, 0)),
        out_shape=jax.ShapeDtypeStruct((tokens, D_MODEL), F32),
        scratch_shapes=[
            pltpu.VMEM((FFN_HALO + tm, D_MODEL), BF16),
            pltpu.VMEM((FFN_U_BUFFERS, FFN_HALO + tm, FFN_TN), F32),
            pltpu.VMEM((tm, D_FF), BF16),
        ],
        compiler_params=pltpu.CompilerParams(
            dimension_semantics=("arbitrary",), vmem_limit_bytes=48 * MIB),
        name="conv_ffn",
    )(x1, ln2_g, w_up, conv_w, conv_b, w_down, lnf_g)


def _lane_bcast(v):
    return jnp.broadcast_to(v[..., None], v.shape + (LANES,)).astype(F32)


def _group_pad(v):
    v = v.reshape(N_SSM_GROUPS, HEADS_PER_GROUP)
    v = jnp.pad(v, ((0, 0), (0, DT_ROWS_PER_GROUP - HEADS_PER_GROUP)))
    return v.reshape(N_SSM_GROUPS * DT_ROWS_PER_GROUP)


def _layer(x2, batch, seq, ln1_g, w_in, ssm_conv_w, ssm_conv_b, dt_bias, a_log, d_skip,
           attn_norm_g, ssm_norm_g, w_out, ln2_g, w_up, ffn_conv_w, ffn_conv_b, w_down, out_g):
    c_q, c_k, c_v, c_z = 0, D_ATT, 2 * D_ATT, 3 * D_ATT
    c_xs = c_z + D_SSM
    c_b = c_xs + D_SSM
    c_c = c_b + N_SSM_GROUPS * D_STATE
    c_dt = c_c + N_SSM_GROUPS * D_STATE

    w_k = w_in[:, c_k:c_v].astype(BF16)
    w_f_t = jnp.concatenate(
        [w_in[:, c_z:c_xs], w_in[:, c_xs:c_b], w_in[:, c_q:c_k], w_in[:, c_v:c_z],
         w_in[:, c_b:c_c], w_in[:, c_c:c_dt]], axis=1).T.astype(BF16)
    w_dt_t = w_in[:, c_dt:].T.reshape(N_SSM_GROUPS, HEADS_PER_GROUP, D_MODEL)
    w_dt_t = jnp.pad(w_dt_t, ((0, 0), (0, DT_ROWS_PER_GROUP - HEADS_PER_GROUP), (0, 0)))
    w_dt_t = w_dt_t.reshape(N_SSM_GROUPS * DT_ROWS_PER_GROUP, D_MODEL).astype(BF16)

    conv_p = _lane_bcast(jnp.concatenate([ssm_conv_w, ssm_conv_b[None, :]], axis=0))
    k_rm, feat_t, dt_t = _inproj(x2, ln1_g[None, :], w_k, w_f_t, w_dt_t, conv_p, seq)

    att = _attention(feat_t, k_rm, attn_norm_g[None, :], batch, seq)

    pxs = _lane_bcast(jnp.stack([jnp.repeat(d_skip, HEAD_DIM), ssm_norm_g], axis=0))
    pdt = _lane_bcast(jnp.stack([_group_pad(dt_bias), _group_pad(a_log)], axis=0))
    y = _ssd(feat_t, dt_t, pxs, pdt, batch, seq)

    w_out_b = w_out.astype(BF16)
    x1 = _outproj(x2, att, y, w_out_b[:D_ATT], w_out_b[D_ATT:])

    return _ffn(x1, ln2_g[None, :], w_up.astype(BF16), ffn_conv_w, ffn_conv_b[None, :],
                w_down.astype(BF16), out_g[None, :], seq)


def kernel(x, ln1_g, w_in, ssm_conv_w, ssm_conv_b, dt_bias, a_log, d_skip, attn_norm_g,
           ssm_norm_g, w_out, ln2_g, w_up, ffn_conv_w, ffn_conv_b, w_down, lnf_g):
    batch, seq, d_model = x.shape
    depth = ln1_g.shape[0]
    assert d_model == D_MODEL and depth == 1
    assert seq % MOBA_BLOCK == 0 and seq % FFN_TM == 0 and seq % INPROJ_TM == 0
    assert (batch * seq) % OUTPROJ_TM == 0
    x2 = x.reshape(batch * seq, d_model)
    out = _layer(x2, batch, seq, ln1_g[0], w_in[0], ssm_conv_w[0], ssm_conv_b[0], dt_bias[0],
                 a_log[0], d_skip[0], attn_norm_g[0], ssm_norm_g[0], w_out[0], ln2_g[0],
                 w_up[0], ffn_conv_w[0], ffn_conv_b[0], w_down[0], lnf_g)
    return out.reshape(batch, seq, d_model)
```

```python
import functools

import jax
import jax.numpy as jnp
from jax import lax
from jax.experimental import pallas as pl
from jax.experimental.pallas import tpu as pltpu

F32 = jnp.float32
BF16 = jnp.bfloat16
HIGHEST = lax.Precision.HIGHEST

EPS = 1e-6
NEG_INF = -1e30
LOG2_E = 1.4426950408889634

D_MODEL = 1024
HEAD_DIM = 64
D_ATT = 512
N_ATT_HEADS = D_ATT // HEAD_DIM
MOBA_BLOCK = 256
MOBA_TOPK = 3
D_SSM = 1536
N_SSM_GROUPS = 4
HEADS_PER_GROUP = 6
GROUP_ROWS = HEADS_PER_GROUP * HEAD_DIM
D_STATE = 128
SSM_CONV = 4
SSD_CHUNK = 128
D_FF = 2816
FFN_CONV = 3

LANES = 128
SUBLANES = 8
BF16_SUBLANES = 16
DT_ROWS_PER_GROUP = SUBLANES

F_Z = 0
F_XS = F_Z + D_SSM
F_Q = F_XS + D_SSM
F_V = F_Q + D_ATT
F_B = F_V + D_ATT
F_C = F_B + N_SSM_GROUPS * D_STATE
N_FEAT = F_C + N_SSM_GROUPS * D_STATE

NT_DIMS = (((1,), (1,)), ((), ()))

MIB = 1024 * 1024


def _sigmoid(x):
    return 1.0 / (1.0 + jnp.exp(-x))


def _rms_rows(x, gain):
    ms = jnp.mean(x * x, axis=-1, keepdims=True)
    return (x * lax.rsqrt(ms + EPS)) * gain


INPROJ_TM = 512
INPROJ_TN = 256


def _conv_silu_time_on_lanes(cur, tail_rolled, cp_ref, ch0):
    rows, width = cur.shape
    lane = lax.broadcasted_iota(jnp.int32, (1, LANES), 1)
    taps = [cp_ref[k, ch0:ch0 + rows, :] for k in range(SSM_CONV)]
    bias = cp_ref[SSM_CONV, ch0:ch0 + rows, :]
    blocks = [cur[:, c * LANES:(c + 1) * LANES] for c in range(width // LANES)]
    rolled = [tail_rolled] + [[pltpu.roll(b, k, axis=1) for k in range(1, SSM_CONV)] for b in blocks]
    out = []
    for c, block in enumerate(blocks):
        acc = bias + taps[SSM_CONV - 1] * block
        for k in range(1, SSM_CONV):
            shifted = jnp.where(lane >= k, rolled[c + 1][k - 1], rolled[c][k - 1])
            acc = acc + taps[SSM_CONV - 1 - k] * shifted
        out.append(acc * _sigmoid(acc))
    return jnp.concatenate(out, axis=1), rolled[-1]


def _inproj_kernel(x_ref, g_ref, wk_ref, wf_ref, wdt_ref, cp_ref, k_ref, f_ref, dt_ref, tail_s,
                   *, tiles_per_seq):
    tn = INPROJ_TN
    tm = x_ref.shape[0]

    @pl.when(pl.program_id(0) % tiles_per_seq == 0)
    def _():
        tail_s[...] = jnp.zeros(tail_s.shape, F32)

    h = _rms_rows(x_ref[...], g_ref[...]).astype(BF16)
    for n0 in range(0, N_FEAT, tn):
        res = lax.dot_general(wf_ref[n0:n0 + tn, :], h, NT_DIMS, preferred_element_type=F32)
        if n0 < F_XS:
            res = res * _sigmoid(res)
        elif n0 < F_Q or n0 >= F_B:
            ch0 = n0 - F_XS if n0 < F_Q else n0 - F_B + D_SSM
            tail_rolled = [tail_s[k, ch0:ch0 + tn, :] for k in range(SSM_CONV - 1)]
            res, new_tail = _conv_silu_time_on_lanes(res, tail_rolled, cp_ref, ch0)
            for k in range(SSM_CONV - 1):
                tail_s[k, ch0:ch0 + tn, :] = new_tail[k]
        f_ref[n0:n0 + tn, :] = res.astype(BF16)
    k_ref[...] = jnp.dot(h, wk_ref[...], preferred_element_type=F32).astype(BF16)
    dt_ref[...] = lax.dot_general(wdt_ref[...], h, NT_DIMS, preferred_element_type=F32)


def _inproj(x2, ln_g, w_k, w_f_t, w_dt_t, conv_p, seq):
    tokens = x2.shape[0]
    tm = INPROJ_TM
    dt_rows = w_dt_t.shape[0]
    d_xbc = conv_p.shape[1]
    resident = dict(pipeline_mode=pl.Buffered(1))
    return pl.pallas_call(
        functools.partial(_inproj_kernel, tiles_per_seq=seq // tm),
        grid=(tokens // tm,),
        in_specs=[
            pl.BlockSpec((tm, D_MODEL), lambda i: (i, 0)),
            pl.BlockSpec((1, D_MODEL), lambda i: (0, 0)),
            pl.BlockSpec((D_MODEL, D_ATT), lambda i: (0, 0), **resident),
            pl.BlockSpec((N_FEAT, D_MODEL), lambda i: (0, 0), **resident),
            pl.BlockSpec((dt_rows, D_MODEL), lambda i: (0, 0), **resident),
            pl.BlockSpec((SSM_CONV + 1, d_xbc, LANES), lambda i: (0, 0, 0), **resident),
        ],
        out_specs=[
            pl.BlockSpec((tm, D_ATT), lambda i: (i, 0)),
            pl.BlockSpec((N_FEAT, tm), lambda i: (0, i)),
            pl.BlockSpec((dt_rows, tm), lambda i: (0, i)),
        ],
        out_shape=[
            jax.ShapeDtypeStruct((tokens, D_ATT), BF16),
            jax.ShapeDtypeStruct((N_FEAT, tokens), BF16),
            jax.ShapeDtypeStruct((dt_rows, tokens), F32),
        ],
        scratch_shapes=[pltpu.VMEM((SSM_CONV - 1, d_xbc, LANES), F32)],
        compiler_params=pltpu.CompilerParams(
            dimension_semantics=("arbitrary",), vmem_limit_bytes=52 * MIB),
        name="inproj",
    )(x2, ln_g, w_k, w_f_t, w_dt_t, conv_p)


def _alibi_slope(head):
    return float(2.0 ** (-8.0 * (head + 1) / N_ATT_HEADS))


def _attn_kernel(q_ref, k_ref, kown_ref, v_ref, vown_ref, g_ref, o_ref,
                 kmean_s, qm_s, sel_s, alibi_s, t_s, tmax_s, m_s, acc_s, out_s, *, nb):
    i = pl.program_id(1)
    blk = MOBA_BLOCK
    pair = 2 * HEAD_DIM

    @pl.when(i == 0)
    def _():
        kmean_s[...] = jnp.zeros(kmean_s.shape, F32)
        for j in range(nb):
            kb = k_ref[j * blk:(j + 1) * blk, :].astype(F32)
            kmean_s[j:j + 1, :] = jnp.mean(kb, axis=0, keepdims=True)
        key_off = lax.broadcasted_iota(jnp.int32, (blk, blk), 0).astype(F32)
        for head in range(N_ATT_HEADS):
            alibi_s[head] = (_alibi_slope(head) * LOG2_E) * key_off

    cand = lax.broadcasted_iota(jnp.int32, (kmean_s.shape[0], blk), 0)
    valid = cand < i
    feat = lax.broadcasted_iota(jnp.int32, (pair, blk), 0)

    pair_lane = lax.broadcasted_iota(jnp.int32, (kmean_s.shape[0], pair), 1)
    for hp in range(N_ATT_HEADS // 2):
        q_raw = q_ref[hp * pair:(hp + 1) * pair, :]
        q2 = q_raw.astype(F32) * (HEAD_DIM ** -0.5 * LOG2_E)
        kmean_pair = kmean_s[:, hp * pair:(hp + 1) * pair]
        for sub in range(2):
            head = 2 * hp + sub
            in_head = (feat < HEAD_DIM) if sub == 0 else (feat >= HEAD_DIM)
            qm_s[head] = jnp.where(in_head, q2, 0.0).astype(BF16)

            km = jnp.where((pair_lane < HEAD_DIM) if sub == 0 else (pair_lane >= HEAD_DIM), kmean_pair, 0.0)
            km_hi = km.astype(BF16)
            rest = km - km_hi.astype(F32)
            km_mid = rest.astype(BF16)
            km_lo = (rest - km_mid.astype(F32)).astype(BF16)
            gate = (jnp.dot(km_lo, q_raw, preferred_element_type=F32)
                    + jnp.dot(km_mid, q_raw, preferred_element_type=F32)
                    + jnp.dot(km_hi, q_raw, preferred_element_type=F32))
            gate = jnp.where(valid, gate, NEG_INF)
            rank = jnp.zeros(gate.shape, F32)
            for jp in range(nb):
                gj = gate[jp:jp + 1, :]
                ge = jnp.where(gj >= gate, 1.0, 0.0)
                gt = jnp.where(gj > gate, 1.0, 0.0)
                rank = rank + jnp.where(cand > jp, ge, gt)
            sel_s[head] = jnp.where(valid, jnp.where(rank < MOBA_TOPK, 0.0, NEG_INF), NEG_INF)

    m_s[...] = jnp.full(m_s.shape, NEG_INF, F32)
    acc_s[...] = jnp.zeros(acc_s.shape, F32)

    causal = (lax.broadcasted_iota(jnp.int32, (blk, blk), 0)
              <= lax.broadcasted_iota(jnp.int32, (blk, blk), 1))

    all_heads = tuple(range(N_ATT_HEADS))

    def scores(tile, slot, heads=all_heads):
        for head in heads:
            cols = slice((head // 2) * pair, (head // 2 + 1) * pair)
            kj = kown_ref[:, cols] if tile is None else k_ref[tile * blk:(tile + 1) * blk, cols]
            t = jnp.dot(kj, qm_s[head], preferred_element_type=F32) + alibi_s[head]
            if tile is None:
                t = jnp.where(causal, t, NEG_INF)
            t_s[slot, head] = t
            quarter = blk // 4
            tq = jnp.maximum(jnp.maximum(t[0:quarter], t[quarter:2 * quarter]),
                             jnp.maximum(t[2 * quarter:3 * quarter], t[3 * quarter:]))
            tmax_s[slot, head, 0:1, :] = jnp.max(tq, axis=0, keepdims=True)

    ones_rows = jnp.ones((BF16_SUBLANES, blk), BF16)

    def accumulate(tile, slot, heads=all_heads):
        for head in heads:
            rows = slice(head * HEAD_DIM, (head + 1) * HEAD_DIM)
            m_old = m_s[head, 0:1, :]
            if tile is None:
                m_new = jnp.maximum(m_old, tmax_s[slot, head, 0:1, :])
                shift = m_new
                v = vown_ref[rows, :]
            else:
                row = (sel_s[head, tile:tile + 1, :]
                       + (_alibi_slope(head) * LOG2_E * blk) * (tile - i).astype(F32))
                m_new = jnp.maximum(m_old, tmax_s[slot, head, 0:1, :] + row)
                shift = m_new - row
                v = v_ref[rows, tile * blk:(tile + 1) * blk]
            p = jnp.exp2(t_s[slot, head] - shift)
            alpha = jnp.exp2(m_old - m_new)
            m_s[head, 0:1, :] = m_new
            vj = jnp.concatenate([v, ones_rows], axis=0)
            acc_s[head] = alpha * acc_s[head] + jnp.dot(vj, p.astype(BF16), preferred_element_type=F32)

    scores(None, 0)
    for j in range(nb - 1):
        @pl.when(j < i)
        def _(j=j):
            for head in all_heads:
                accumulate(None if j == 0 else j - 1, j % 2, (head,))
                scores(j, (j + 1) % 2, (head,))

    @pl.when(i == 0)
    def _():
        accumulate(None, 0)

    for j in range(nb - 1):
        @pl.when(i == j + 1)
        def _(j=j):
            accumulate(j, (j + 1) % 2)

    for head in range(N_ATT_HEADS):
        out_s[head * HEAD_DIM:(head + 1) * HEAD_DIM, :] = (
            acc_s[head, 0:HEAD_DIM, :] / acc_s[head, HEAD_DIM:HEAD_DIM + 1, :])
    att = out_s[...].T
    o_ref[...] = _rms_rows(att, g_ref[...]).astype(BF16)


def _attention(feat_t, k_rm, attn_g, batch, seq):
    nb = seq // MOBA_BLOCK
    cands = max(SUBLANES, nb)
    q_blk = F_Q // D_ATT
    v_blk = F_V // D_ATT
    return pl.pallas_call(
        functools.partial(_attn_kernel, nb=nb),
        grid=(batch, nb),
        in_specs=[
            pl.BlockSpec((D_ATT, MOBA_BLOCK), lambda b, i: (q_blk, b * nb + i)),
            pl.BlockSpec((seq, D_ATT), lambda b, i: (b, 0)),
            pl.BlockSpec((MOBA_BLOCK, D_ATT), lambda b, i: (b * nb + i, 0)),
            pl.BlockSpec((D_ATT, seq), lambda b, i: (v_blk, b)),
            pl.BlockSpec((D_ATT, MOBA_BLOCK), lambda b, i: (v_blk, b * nb + i)),
            pl.BlockSpec((1, D_ATT), lambda b, i: (0, 0)),
        ],
        out_specs=pl.BlockSpec((MOBA_BLOCK, D_ATT), lambda b, i: (b * nb + i, 0)),
        out_shape=jax.ShapeDtypeStruct((batch * seq, D_ATT), BF16),
        scratch_shapes=[
            pltpu.VMEM((cands, D_ATT), F32),
            pltpu.VMEM((N_ATT_HEADS, 2 * HEAD_DIM, MOBA_BLOCK), BF16),
            pltpu.VMEM((N_ATT_HEADS, cands, MOBA_BLOCK), F32),
            pltpu.VMEM((N_ATT_HEADS, MOBA_BLOCK, MOBA_BLOCK), F32),
            pltpu.VMEM((2, N_ATT_HEADS, MOBA_BLOCK, MOBA_BLOCK), F32),
            pltpu.VMEM((2, N_ATT_HEADS, SUBLANES, MOBA_BLOCK), F32),
            pltpu.VMEM((N_ATT_HEADS, SUBLANES, MOBA_BLOCK), F32),
            pltpu.VMEM((N_ATT_HEADS, HEAD_DIM + BF16_SUBLANES, MOBA_BLOCK), F32),
            pltpu.VMEM((D_ATT, MOBA_BLOCK), F32),
        ],
        compiler_params=pltpu.CompilerParams(
            dimension_semantics=("arbitrary", "arbitrary"), vmem_limit_bytes=32 * MIB),
        name="moba_attention",
    )(feat_t, k_rm, k_rm, feat_t, feat_t, attn_g)


P_DSKIP, P_NORMG = 0, 1
N_XS_PARAMS = 2


def _ssd_kernel(zs_ref, xs_ref, b_ref, c_ref, dt_ref, pxs_ref, pdt_ref, y_ref,
                bt_s, cbt_s, ylocal_s, newst_s, state_s, *, seq):
    L = SSD_CHUNK
    n_chunks = seq // L
    r_i = lax.broadcasted_iota(jnp.int32, (L, L), 0)
    c_i = lax.broadcasted_iota(jnp.int32, (L, L), 1)
    upper = jnp.where(r_i <= c_i, 1.0, 0.0).astype(F32)
    s_le_l = r_i <= c_i

    dt_bias = pdt_ref[0]
    a_neg = -jnp.exp(pdt_ref[1])

    raw = jnp.concatenate([dt_ref[:, c * L:(c + 1) * L] + dt_bias for c in range(n_chunks)], axis=0)
    dt_all = jnp.maximum(raw, 0.0) + jnp.log1p(jnp.exp(-jnp.abs(raw)))
    a_dt = dt_all * jnp.concatenate([a_neg] * n_chunks, axis=0)
    acs_all = jnp.dot(a_dt, upper, precision=HIGHEST, preferred_element_type=F32)
    acs_log2 = acs_all * LOG2_E
    acs_log2_t = acs_log2.T
    acs_end = jnp.broadcast_to(acs_all[:, L - 1:L], acs_all.shape)
    to_end_all = jnp.exp(acs_end - acs_all)
    out_decay_all = jnp.exp(acs_all)
    carry_decay_all = jnp.exp(acs_end)

    def rows_of(c):
        return slice(c * DT_ROWS_PER_GROUP, (c + 1) * DT_ROWS_PER_GROUP)

    def head_rows(k):
        return slice(k * HEAD_DIM, (k + 1) * HEAD_DIM)

    for c in range(n_chunks):
        b_t = b_ref[:, c * L:(c + 1) * L].astype(F32).T.astype(BF16)
        bt_s[c] = b_t
        cbt_s[c] = jnp.dot(b_t, c_ref[:, c * L:(c + 1) * L], preferred_element_type=F32)

    for c in range(n_chunks):
        cols = slice(c * L, (c + 1) * L)
        xs = xs_ref[:, cols].astype(F32)
        dt = dt_all[rows_of(c)]
        acs_row = acs_log2[rows_of(c)]
        acs_col = acs_log2_t[:, rows_of(c)]
        to_end = to_end_all[rows_of(c)]
        cb_t = cbt_s[c]
        y_diag, x_state = [], []
        for k in range(HEADS_PER_GROUP):
            x_dt = xs[head_rows(k), :] * dt[k:k + 1, :]
            seg = acs_row[k:k + 1, :] - acs_col[:, k:k + 1]
            decay = jnp.exp2(jnp.where(s_le_l, seg, NEG_INF))
            m_t = (cb_t * decay).astype(BF16)
            y_diag.append(jnp.dot(x_dt.astype(BF16), m_t, preferred_element_type=F32))
            x_state.append((x_dt * to_end[k:k + 1, :]).astype(BF16))
        ylocal_s[:, cols] = jnp.concatenate(y_diag, axis=0) + pxs_ref[P_DSKIP] * xs
        newst_s[c] = jnp.dot(jnp.concatenate(x_state, axis=0), bt_s[c], preferred_element_type=F32)

    state_s[...] = jnp.zeros(state_s.shape, F32)
    for c in range(n_chunks):
        cols = slice(c * L, (c + 1) * L)
        out_decay = jnp.concatenate(
            [jnp.broadcast_to(out_decay_all[rows_of(c)][k:k + 1, :], (HEAD_DIM, L))
             for k in range(HEADS_PER_GROUP)], axis=0)
        carry_decay = jnp.concatenate(
            [jnp.broadcast_to(carry_decay_all[rows_of(c)][k:k + 1, :], (HEAD_DIM, L))
             for k in range(HEADS_PER_GROUP)], axis=0)
        prev = state_s[...]
        y_off = jnp.dot(prev.astype(BF16), c_ref[:, cols], preferred_element_type=F32) * out_decay
        state_s[...] = prev * carry_decay + newst_s[c]
        y = (ylocal_s[:, cols] + y_off) * zs_ref[:, cols].astype(F32)
        ms = jnp.mean(y * y, axis=0, keepdims=True)
        yn = (y * lax.rsqrt(ms + EPS)) * pxs_ref[P_NORMG]
        y_ref[cols, :] = yn.T.astype(BF16)


def _ssd(feat_t, dt_t, pxs, pdt, batch, seq):
    g_rows = GROUP_ROWS
    z_blk0 = F_Z // g_rows
    xs_blk0 = F_XS // g_rows
    b_blk0 = F_B // D_STATE
    c_blk0 = F_C // D_STATE
    return pl.pallas_call(
        functools.partial(_ssd_kernel, seq=seq),
        grid=(batch, N_SSM_GROUPS),
        in_specs=[
            pl.BlockSpec((g_rows, seq), lambda b, g: (z_blk0 + g, b)),
            pl.BlockSpec((g_rows, seq), lambda b, g: (xs_blk0 + g, b)),
            pl.BlockSpec((D_STATE, seq), lambda b, g: (b_blk0 + g, b)),
            pl.BlockSpec((D_STATE, seq), lambda b, g: (c_blk0 + g, b)),
            pl.BlockSpec((DT_ROWS_PER_GROUP, seq), lambda b, g: (g, b)),
            pl.BlockSpec((N_XS_PARAMS, g_rows, LANES), lambda b, g: (0, g, 0)),
            pl.BlockSpec((2, DT_ROWS_PER_GROUP, LANES), lambda b, g: (0, g, 0)),
        ],
        out_specs=pl.BlockSpec((seq, g_rows), lambda b, g: (b, g)),
        out_shape=jax.ShapeDtypeStruct((batch * seq, D_SSM), BF16),
        scratch_shapes=[
            pltpu.VMEM((seq // SSD_CHUNK, SSD_CHUNK, D_STATE), BF16),
            pltpu.VMEM((seq // SSD_CHUNK, SSD_CHUNK, SSD_CHUNK), F32),
            pltpu.VMEM((g_rows, seq), F32),
            pltpu.VMEM((seq // SSD_CHUNK, g_rows, D_STATE), F32),
            pltpu.VMEM((g_rows, D_STATE), F32),
        ],
        compiler_params=pltpu.CompilerParams(
            dimension_semantics=("arbitrary", "arbitrary"), vmem_limit_bytes=40 * MIB),
        name="ssd_mixer",
    )(feat_t, feat_t, feat_t, feat_t, dt_t, pxs, pdt)


OUTPROJ_TM = 1024


def _outproj_kernel(x_ref, a_ref, y_ref, wa_ref, wy_ref, o_ref):
    o_ref[...] = (x_ref[...]
                  + jnp.dot(a_ref[...], wa_ref[...], preferred_element_type=F32)
                  + jnp.dot(y_ref[...], wy_ref[...], preferred_element_type=F32))


def _outproj(x2, att, y, w_att, w_ssm):
    tokens = x2.shape[0]
    tm = OUTPROJ_TM
    resident = dict(pipeline_mode=pl.Buffered(1))
    return pl.pallas_call(
        _outproj_kernel,
        grid=(tokens // tm,),
        in_specs=[
            pl.BlockSpec((tm, D_MODEL), lambda i: (i, 0)),
            pl.BlockSpec((tm, D_ATT), lambda i: (i, 0)),
            pl.BlockSpec((tm, D_SSM), lambda i: (i, 0)),
            pl.BlockSpec((D_ATT, D_MODEL), lambda i: (0, 0), **resident),
            pl.BlockSpec((D_SSM, D_MODEL), lambda i: (0, 0), **resident),
        ],
        out_specs=pl.BlockSpec((tm, D_MODEL), lambda i: (i, 0)),
        out_shape=jax.ShapeDtypeStruct((tokens, D_MODEL), F32),
        compiler_params=pltpu.CompilerParams(
            dimension_semantics=("arbitrary",), vmem_limit_bytes=40 * MIB),
        name="outproj",
    )(x2, att, y, w_att, w_ssm)


FFN_TM = 512
FFN_TN = 256
FFN_HALO = BF16_SUBLANES
FFN_U_BUFFERS = 4


def _ffn_kernel(x_ref, g2_ref, wup_ref, cw_ref, cb_ref, wdn_ref, gf_ref, o_ref,
                h_s, u_s, act_s, *, tiles_per_seq):
    i = pl.program_id(0)
    tm = FFN_TM
    halo = FFN_HALO

    @pl.when(i % tiles_per_seq == 0)
    def _():
        h_s[0:halo, :] = jnp.zeros((halo, D_MODEL), BF16)

    x = x_ref[...]
    h_s[halo:, :] = _rms_rows(x, g2_ref[...]).astype(BF16)
    h_ext = h_s[...]

    def conv(col0, width, buf):
        u_s[buf] = jnp.dot(h_ext, wup_ref[:, col0:col0 + width], preferred_element_type=F32)
        acc = cb_ref[:, col0:col0 + width]
        for k in range(FFN_CONV):
            shift = FFN_CONV - 1 - k
            acc = acc + cw_ref[k:k + 1, col0:col0 + width] * u_s[buf, halo - shift:halo - shift + tm, :]
        return acc

    for c, n0 in enumerate(range(0, D_FF, FFN_TN)):
        gate = conv(n0, FFN_TN, (2 * c) % FFN_U_BUFFERS)
        val = conv(D_FF + n0, FFN_TN, (2 * c + 1) % FFN_U_BUFFERS)
        act_s[:, n0:n0 + FFN_TN] = ((gate * _sigmoid(gate)) * val).astype(BF16)

    h_s[0:halo, :] = h_s[tm:tm + halo, :]
    half = (D_FF // (2 * LANES)) * LANES
    x2 = (x + jnp.dot(act_s[:, :half], wdn_ref[:half, :], preferred_element_type=F32)
          + jnp.dot(act_s[:, half:], wdn_ref[half:, :], preferred_element_type=F32))
    o_ref[...] = _rms_rows(x2, gf_ref[...])


def _ffn(x1, ln2_g, w_up, conv_w, conv_b, w_down, lnf_g, seq):
    tokens = x1.shape[0]
    tm = FFN_TM
    resident = dict(pipeline_mode=pl.Buffered(1))
    return pl.pallas_call(
        functools.partial(_ffn_kernel, tiles_per_seq=seq // tm),
        grid=(tokens // tm,),
        in_specs=[
            pl.BlockSpec((tm, D_MODEL), lambda i: (i, 0)),
            pl.BlockSpec((1, D_MODEL), lambda i: (0, 0)),
            pl.BlockSpec((D_MODEL, 2 * D_FF), lambda i: (0, 0), **resident),
            pl.BlockSpec((FFN_CONV, 2 * D_FF), lambda i: (0, 0)),
            pl.BlockSpec((1, 2 * D_FF), lambda i: (0, 0)),
            pl.BlockSpec((D_FF, D_MODEL), lambda i: (0, 0), **resident),
            pl.BlockSpec((1, D_MODEL), lambda i: (0, 0)),
        ],
        out_specs=pl.BlockSpec((tm, D_MODEL), lambda i: (i, 0)),
        out_shape=jax.ShapeDtypeStruct((tokens, D_MODEL), F32),
        scratch_shapes=[
            pltpu.VMEM((FFN_HALO + tm, D_MODEL), BF16),
            pltpu.VMEM((FFN_U_BUFFERS, FFN_HALO + tm, FFN_TN), F32),
            pltpu.VMEM((tm, D_FF), BF16),
        ],
        compiler_params=pltpu.CompilerParams(
            dimension_semantics=("arbitrary",), vmem_limit_bytes=48 * MIB),
        name="conv_ffn",
    )(x1, ln2_g, w_up, conv_w, conv_b, w_down, lnf_g)


def _lane_bcast(v):
    return jnp.broadcast_to(v[..., None], v.shape + (LANES,)).astype(F32)


def _group_pad(v):
    v = v.reshape(N_SSM_GROUPS, HEADS_PER_GROUP)
    v = jnp.pad(v, ((0, 0), (0, DT_ROWS_PER_GROUP - HEADS_PER_GROUP)))
    return v.reshape(N_SSM_GROUPS * DT_ROWS_PER_GROUP)


def _layer(x2, batch, seq, ln1_g, w_in, ssm_conv_w, ssm_conv_b, dt_bias, a_log, d_skip,
           attn_norm_g, ssm_norm_g, w_out, ln2_g, w_up, ffn_conv_w, ffn_conv_b, w_down, out_g):
    c_q, c_k, c_v, c_z = 0, D_ATT, 2 * D_ATT, 3 * D_ATT
    c_xs = c_z + D_SSM
    c_b = c_xs + D_SSM
    c_c = c_b + N_SSM_GROUPS * D_STATE
    c_dt = c_c + N_SSM_GROUPS * D_STATE

    w_k = w_in[:, c_k:c_v].astype(BF16)
    w_f_t = jnp.concatenate(
        [w_in[:, c_z:c_xs], w_in[:, c_xs:c_b], w_in[:, c_q:c_k], w_in[:, c_v:c_z],
         w_in[:, c_b:c_c], w_in[:, c_c:c_dt]], axis=1).T.astype(BF16)
    w_dt_t = w_in[:, c_dt:].T.reshape(N_SSM_GROUPS, HEADS_PER_GROUP, D_MODEL)
    w_dt_t = jnp.pad(w_dt_t, ((0, 0), (0, DT_ROWS_PER_GROUP - HEADS_PER_GROUP), (0, 0)))
    w_dt_t = w_dt_t.reshape(N_SSM_GROUPS * DT_ROWS_PER_GROUP, D_MODEL).astype(BF16)

    conv_p = _lane_bcast(jnp.concatenate([ssm_conv_w, ssm_conv_b[None, :]], axis=0))
    k_rm, feat_t, dt_t = _inproj(x2, ln1_g[None, :], w_k, w_f_t, w_dt_t, conv_p, seq)

    att = _attention(feat_t, k_rm, attn_norm_g[None, :], batch, seq)

    pxs = _lane_bcast(jnp.stack([jnp.repeat(d_skip, HEAD_DIM), ssm_norm_g], axis=0))
    pdt = _lane_bcast(jnp.stack([_group_pad(dt_bias), _group_pad(a_log)], axis=0))
    y = _ssd(feat_t, dt_t, pxs, pdt, batch, seq)

    w_out_b = w_out.astype(BF16)
    x1 = _outproj(x2, att, y, w_out_b[:D_ATT], w_out_b[D_ATT:])

    return _ffn(x1, ln2_g[None, :], w_up.astype(BF16), ffn_conv_w, ffn_conv_b[None, :],
                w_down.astype(BF16), out_g[None, :], seq)


def kernel(x, ln1_g, w_in, ssm_conv_w, ssm_conv_b, dt_bias, a_log, d_skip, attn_norm_g,
           ssm_norm_g, w_out, ln2_g, w_up, ffn_conv_w, ffn_conv_b, w_down, lnf_g):
    batch, seq, d_model = x.shape
    depth = ln1_g.shape[0]
    assert d_model == D_MODEL and depth == 1
    assert seq % MOBA_BLOCK == 0 and seq % FFN_TM == 0 and seq % INPROJ_TM == 0
    assert (batch * seq) % OUTPROJ_TM == 0
    x2 = x.reshape(batch * seq, d_model)
    out = _layer(x2, batch, seq, ln1_g[0], w_in[0], ssm_conv_w[0], ssm_conv_b[0], dt_bias[0],
                 a_log[0], d_skip[0], attn_norm_g[0], ssm_norm_g[0], w_out[0], ln2_g[0],
                 w_up[0], ffn_conv_w[0], ffn_conv_b[0], w_down[0], lnf_g)
    return out.reshape(batch, seq, d_model)
```

```python
import functools

import jax
import jax.numpy as jnp
from jax import lax
from jax.experimental import pallas as pl
from jax.experimental.pallas import tpu as pltpu

F32 = jnp.float32
BF16 = jnp.bfloat16
HIGHEST = lax.Precision.HIGHEST

EPS = 1e-6
NEG_INF = -1e30
LOG2_E = 1.4426950408889634

D_MODEL = 1024
HEAD_DIM = 64
D_ATT = 512
N_ATT_HEADS = D_ATT // HEAD_DIM
MOBA_BLOCK = 256
MOBA_TOPK = 3
D_SSM = 1536
N_SSM_GROUPS = 4
HEADS_PER_GROUP = 6
GROUP_ROWS = HEADS_PER_GROUP * HEAD_DIM
D_STATE = 128
SSM_CONV = 4
SSD_CHUNK = 128
D_FF = 2816
FFN_CONV = 3

LANES = 128
SUBLANES = 8
BF16_SUBLANES = 16
DT_ROWS_PER_GROUP = SUBLANES

F_Z = 0
F_XS = F_Z + D_SSM
F_Q = F_XS + D_SSM
F_V = F_Q + D_ATT
F_B = F_V + D_ATT
F_C = F_B + N_SSM_GROUPS * D_STATE
N_FEAT = F_C + N_SSM_GROUPS * D_STATE

NT_DIMS = (((1,), (1,)), ((), ()))

MIB = 1024 * 1024


def _sigmoid(x):
    return 1.0 / (1.0 + jnp.exp(-x))


def _rms_rows(x, gain):
    ms = jnp.mean(x * x, axis=-1, keepdims=True)
    return (x * lax.rsqrt(ms + EPS)) * gain


INPROJ_TM = 512
INPROJ_TN = 256


def _conv_silu_time_on_lanes(cur, tail_rolled, cp_ref, ch0):
    rows, width = cur.shape
    lane = lax.broadcasted_iota(jnp.int32, (1, LANES), 1)
    taps = [cp_ref[k, ch0:ch0 + rows, :] for k in range(SSM_CONV)]
    bias = cp_ref[SSM_CONV, ch0:ch0 + rows, :]
    blocks = [cur[:, c * LANES:(c + 1) * LANES] for c in range(width // LANES)]
    rolled = [tail_rolled] + [[pltpu.roll(b, k, axis=1) for k in range(1, SSM_CONV)] for b in blocks]
    out = []
    for c, block in enumerate(blocks):
        acc = bias + taps[SSM_CONV - 1] * block
        for k in range(1, SSM_CONV):
            shifted = jnp.where(lane >= k, rolled[c + 1][k - 1], rolled[c][k - 1])
            acc = acc + taps[SSM_CONV - 1 - k] * shifted
        out.append(acc * _sigmoid(acc))
    return jnp.concatenate(out, axis=1), rolled[-1]


def _inproj_kernel(x_ref, g_ref, wk_ref, wf_ref, wdt_ref, cp_ref, k_ref, f_ref, dt_ref, tail_s,
                   *, tiles_per_seq):
    tn = INPROJ_TN
    tm = x_ref.shape[0]

    @pl.when(pl.program_id(0) % tiles_per_seq == 0)
    def _():
        tail_s[...] = jnp.zeros(tail_s.shape, F32)

    h = _rms_rows(x_ref[...], g_ref[...]).astype(BF16)
    order = [n0 for n0 in range(0, N_FEAT, tn) if not F_Q <= n0 < F_B]
    order += [n0 for n0 in range(0, N_FEAT, tn) if F_Q <= n0 < F_B]
    for n0 in order:
        res = lax.dot_general(wf_ref[n0:n0 + tn, :], h, NT_DIMS, preferred_element_type=F32)
        if n0 < F_XS:
            res = res * _sigmoid(res)
        elif n0 < F_Q or n0 >= F_B:
            ch0 = n0 - F_XS if n0 < F_Q else n0 - F_B + D_SSM
            tail_rolled = [tail_s[k, ch0:ch0 + tn, :] for k in range(SSM_CONV - 1)]
            res, new_tail = _conv_silu_time_on_lanes(res, tail_rolled, cp_ref, ch0)
            for k in range(SSM_CONV - 1):
                tail_s[k, ch0:ch0 + tn, :] = new_tail[k]
        f_ref[n0:n0 + tn, :] = res.astype(BF16)
    k_ref[...] = jnp.dot(h, wk_ref[...], preferred_element_type=F32).astype(BF16)
    dt_ref[...] = lax.dot_general(wdt_ref[...], h, NT_DIMS, preferred_element_type=F32)


def _inproj(x2, ln_g, w_k, w_f_t, w_dt_t, conv_p, seq):
    tokens = x2.shape[0]
    tm = INPROJ_TM
    dt_rows = w_dt_t.shape[0]
    d_xbc = conv_p.shape[1]
    resident = dict(pipeline_mode=pl.Buffered(1))
    return pl.pallas_call(
        functools.partial(_inproj_kernel, tiles_per_seq=seq // tm),
        grid=(tokens // tm,),
        in_specs=[
            pl.BlockSpec((tm, D_MODEL), lambda i: (i, 0)),
            pl.BlockSpec((1, D_MODEL), lambda i: (0, 0)),
            pl.BlockSpec((D_MODEL, D_ATT), lambda i: (0, 0), **resident),
            pl.BlockSpec((N_FEAT, D_MODEL), lambda i: (0, 0), **resident),
            pl.BlockSpec((dt_rows, D_MODEL), lambda i: (0, 0), **resident),
            pl.BlockSpec((SSM_CONV + 1, d_xbc, LANES), lambda i: (0, 0, 0), **resident),
        ],
        out_specs=[
            pl.BlockSpec((tm, D_ATT), lambda i: (i, 0)),
            pl.BlockSpec((N_FEAT, tm), lambda i: (0, i)),
            pl.BlockSpec((dt_rows, tm), lambda i: (0, i)),
        ],
        out_shape=[
            jax.ShapeDtypeStruct((tokens, D_ATT), BF16),
            jax.ShapeDtypeStruct((N_FEAT, tokens), BF16),
            jax.ShapeDtypeStruct((dt_rows, tokens), F32),
        ],
        scratch_shapes=[pltpu.VMEM((SSM_CONV - 1, d_xbc, LANES), F32)],
        compiler_params=pltpu.CompilerParams(
            dimension_semantics=("arbitrary",), vmem_limit_bytes=52 * MIB),
        name="inproj",
    )(x2, ln_g, w_k, w_f_t, w_dt_t, conv_p)


ATTN_SCORE_LEAD = 2


def _alibi_slope(head):
    return float(2.0 ** (-8.0 * (head + 1) / N_ATT_HEADS))


def _attn_kernel(q_ref, k_ref, kown_ref, v_ref, vown_ref, g_ref, o_ref,
                 kmean_s, qm_s, sel_s, alibi_s, t_s, tmax_s, m_s, acc_s, out_s, *, nb):
    i = pl.program_id(1)
    blk = MOBA_BLOCK
    pair = 2 * HEAD_DIM

    @pl.when(i == 0)
    def _():
        kmean_s[...] = jnp.zeros(kmean_s.shape, F32)
        for j in range(nb):
            kb = k_ref[j * blk:(j + 1) * blk, :].astype(F32)
            kmean_s[j:j + 1, :] = jnp.mean(kb, axis=0, keepdims=True)
        key_off = lax.broadcasted_iota(jnp.int32, (blk, blk), 0).astype(F32)
        for head in range(N_ATT_HEADS):
            alibi_s[head] = (_alibi_slope(head) * LOG2_E) * key_off

    cand = lax.broadcasted_iota(jnp.int32, (kmean_s.shape[0], blk), 0)
    valid = cand < i
    feat = lax.broadcasted_iota(jnp.int32, (pair, blk), 0)

    pair_lane = lax.broadcasted_iota(jnp.int32, (kmean_s.shape[0], pair), 1)
    for hp in range(N_ATT_HEADS // 2):
        q_raw = q_ref[hp * pair:(hp + 1) * pair, :]
        q2 = q_raw.astype(F32) * (HEAD_DIM ** -0.5 * LOG2_E)
        kmean_pair = kmean_s[:, hp * pair:(hp + 1) * pair]
        for sub in range(2):
            head = 2 * hp + sub
            in_head = (feat < HEAD_DIM) if sub == 0 else (feat >= HEAD_DIM)
            qm_s[head] = jnp.where(in_head, q2, 0.0).astype(BF16)

            km = jnp.where((pair_lane < HEAD_DIM) if sub == 0 else (pair_lane >= HEAD_DIM), kmean_pair, 0.0)
            km_hi = km.astype(BF16)
            rest = km - km_hi.astype(F32)
            km_mid = rest.astype(BF16)
            km_lo = (rest - km_mid.astype(F32)).astype(BF16)
            gate = (jnp.dot(km_lo, q_raw, preferred_element_type=F32)
                    + jnp.dot(km_mid, q_raw, preferred_element_type=F32)
                    + jnp.dot(km_hi, q_raw, preferred_element_type=F32))
            gate = jnp.where(valid, gate, NEG_INF)
            rank = jnp.zeros(gate.shape, F32)
            for jp in range(nb):
                gj = gate[jp:jp + 1, :]
                ge = jnp.where(gj >= gate, 1.0, 0.0)
                gt = jnp.where(gj > gate, 1.0, 0.0)
                rank = rank + jnp.where(cand > jp, ge, gt)
            sel_s[head] = jnp.where(valid, jnp.where(rank < MOBA_TOPK, 0.0, NEG_INF), NEG_INF)

    m_s[...] = jnp.full(m_s.shape, NEG_INF, F32)
    acc_s[...] = jnp.zeros(acc_s.shape, F32)

    causal = (lax.broadcasted_iota(jnp.int32, (blk, blk), 0)
              <= lax.broadcasted_iota(jnp.int32, (blk, blk), 1))

    all_heads = tuple(range(N_ATT_HEADS))

    def scores(tile, slot, heads=all_heads):
        for head in heads:
            cols = slice((head // 2) * pair, (head // 2 + 1) * pair)
            kj = kown_ref[:, cols] if tile is None else k_ref[tile * blk:(tile + 1) * blk, cols]
            t = jnp.dot(kj, qm_s[head], preferred_element_type=F32) + alibi_s[head]
            if tile is None:
                t = jnp.where(causal, t, NEG_INF)
            t_s[slot, head] = t
            quarter = blk // 4
            tq = jnp.maximum(jnp.maximum(t[0:quarter], t[quarter:2 * quarter]),
                             jnp.maximum(t[2 * quarter:3 * quarter], t[3 * quarter:]))
            tmax_s[slot, head, 0:1, :] = jnp.max(tq, axis=0, keepdims=True)

    ones_rows = jnp.ones((BF16_SUBLANES, blk), BF16)

    def accumulate(tile, slot, heads=all_heads):
        for head in heads:
            rows = slice(head * HEAD_DIM, (head + 1) * HEAD_DIM)
            m_old = m_s[head, 0:1, :]
            if tile is None:
                m_new = jnp.maximum(m_old, tmax_s[slot, head, 0:1, :])
                shift = m_new
                v = vown_ref[rows, :]
            else:
                row = (sel_s[head, tile:tile + 1, :]
                       + (_alibi_slope(head) * LOG2_E * blk) * (tile - i).astype(F32))
                m_new = jnp.maximum(m_old, tmax_s[slot, head, 0:1, :] + row)
                shift = m_new - row
                v = v_ref[rows, tile * blk:(tile + 1) * blk]
            p = jnp.exp2(t_s[slot, head] - shift)
            alpha = jnp.exp2(m_old - m_new)
            m_s[head, 0:1, :] = m_new
            vj = jnp.concatenate([v, ones_rows], axis=0)
            acc_s[head] = alpha * acc_s[head] + jnp.dot(vj, p.astype(BF16), preferred_element_type=F32)

    scores(None, 0)
    for j in range(nb - 1):
        @pl.when(j < i)
        def _(j=j):
            prev = None if j == 0 else j - 1
            scores(j, (j + 1) % 2, all_heads[:ATTN_SCORE_LEAD])
            for head in all_heads:
                accumulate(prev, j % 2, (head,))
                scores(j, (j + 1) % 2, all_heads[head + ATTN_SCORE_LEAD:head + ATTN_SCORE_LEAD + 1])

    @pl.when(i == 0)
    def _():
        accumulate(None, 0)

    for j in range(nb - 1):
        @pl.when(i == j + 1)
        def _(j=j):
            accumulate(j, (j + 1) % 2)

    for head in range(N_ATT_HEADS):
        out_s[head * HEAD_DIM:(head + 1) * HEAD_DIM, :] = (
            acc_s[head, 0:HEAD_DIM, :] / acc_s[head, HEAD_DIM:HEAD_DIM + 1, :])
    att = out_s[...].T
    o_ref[...] = _rms_rows(att, g_ref[...]).astype(BF16)


def _attention(feat_t, k_rm, attn_g, batch, seq):
    nb = seq // MOBA_BLOCK
    cands = max(SUBLANES, nb)
    q_blk = F_Q // D_ATT
    v_blk = F_V // D_ATT
    return pl.pallas_call(
        functools.partial(_attn_kernel, nb=nb),
        grid=(batch, nb),
        in_specs=[
            pl.BlockSpec((D_ATT, MOBA_BLOCK), lambda b, i: (q_blk, b * nb + i)),
            pl.BlockSpec((seq, D_ATT), lambda b, i: (b, 0)),
            pl.BlockSpec((MOBA_BLOCK, D_ATT), lambda b, i: (b * nb + i, 0)),
            pl.BlockSpec((D_ATT, seq), lambda b, i: (v_blk, b)),
            pl.BlockSpec((D_ATT, MOBA_BLOCK), lambda b, i: (v_blk, b * nb + i)),
            pl.BlockSpec((1, D_ATT), lambda b, i: (0, 0)),
        ],
        out_specs=pl.BlockSpec((MOBA_BLOCK, D_ATT), lambda b, i: (b * nb + i, 0)),
        out_shape=jax.ShapeDtypeStruct((batch * seq, D_ATT), BF16),
        scratch_shapes=[
            pltpu.VMEM((cands, D_ATT), F32),
            pltpu.VMEM((N_ATT_HEADS, 2 * HEAD_DIM, MOBA_BLOCK), BF16),
            pltpu.VMEM((N_ATT_HEADS, cands, MOBA_BLOCK), F32),
            pltpu.VMEM((N_ATT_HEADS, MOBA_BLOCK, MOBA_BLOCK), F32),
            pltpu.VMEM((2, N_ATT_HEADS, MOBA_BLOCK, MOBA_BLOCK), F32),
            pltpu.VMEM((2, N_ATT_HEADS, SUBLANES, MOBA_BLOCK), F32),
            pltpu.VMEM((N_ATT_HEADS, SUBLANES, MOBA_BLOCK), F32),
            pltpu.VMEM((N_ATT_HEADS, HEAD_DIM + BF16_SUBLANES, MOBA_BLOCK), F32),
            pltpu.VMEM((D_ATT, MOBA_BLOCK), F32),
        ],
        compiler_params=pltpu.CompilerParams(
            dimension_semantics=("arbitrary", "arbitrary"), vmem_limit_bytes=32 * MIB),
        name="moba_attention",
    )(feat_t, k_rm, k_rm, feat_t, feat_t, attn_g)


P_DSKIP, P_NORMG = 0, 1
N_XS_PARAMS = 2


def _ssd_kernel(zs_ref, xs_ref, b_ref, c_ref, dt_ref, pxs_ref, pdt_ref, y_ref,
                bt_s, cbt_s, ylocal_s, newst_s, state_s, *, seq):
    L = SSD_CHUNK
    n_chunks = seq // L
    r_i = lax.broadcasted_iota(jnp.int32, (L, L), 0)
    c_i = lax.broadcasted_iota(jnp.int32, (L, L), 1)
    upper = jnp.where(r_i <= c_i, 1.0, 0.0).astype(F32)
    s_le_l = r_i <= c_i

    dt_bias = pdt_ref[0]
    a_neg = -jnp.exp(pdt_ref[1])

    raw = jnp.concatenate([dt_ref[:, c * L:(c + 1) * L] + dt_bias for c in range(n_chunks)], axis=0)
    dt_all = jnp.maximum(raw, 0.0) + jnp.log1p(jnp.exp(-jnp.abs(raw)))
    a_dt = dt_all * jnp.concatenate([a_neg] * n_chunks, axis=0)
    acs_all = jnp.dot(a_dt, upper, precision=HIGHEST, preferred_element_type=F32)
    acs_log2 = acs_all * LOG2_E
    acs_log2_t = acs_log2.T
    acs_end = jnp.broadcast_to(acs_all[:, L - 1:L], acs_all.shape)
    to_end_all = jnp.exp(acs_end - acs_all)
    out_decay_all = jnp.exp(acs_all)
    carry_decay_all = jnp.exp(acs_end)

    def rows_of(c):
        return slice(c * DT_ROWS_PER_GROUP, (c + 1) * DT_ROWS_PER_GROUP)

    def head_rows(k):
        return slice(k * HEAD_DIM, (k + 1) * HEAD_DIM)

    for c in range(n_chunks):
        b_t = b_ref[:, c * L:(c + 1) * L].astype(F32).T.astype(BF16)
        bt_s[c] = b_t
        cbt_s[c] = jnp.dot(b_t, c_ref[:, c * L:(c + 1) * L], preferred_element_type=F32)

    for c in range(n_chunks):
        cols = slice(c * L, (c + 1) * L)
        xs = xs_ref[:, cols].astype(F32)
        dt = dt_all[rows_of(c)]
        acs_row = acs_log2[rows_of(c)]
        acs_col = acs_log2_t[:, rows_of(c)]
        to_end = to_end_all[rows_of(c)]
        cb_t = cbt_s[c]
        y_diag, x_state = [], []
        for k in range(HEADS_PER_GROUP):
            x_dt = xs[head_rows(k), :] * dt[k:k + 1, :]
            seg = acs_row[k:k + 1, :] - acs_col[:, k:k + 1]
            decay = jnp.exp2(jnp.where(s_le_l, seg, NEG_INF))
            m_t = (cb_t * decay).astype(BF16)
            y_diag.append(jnp.dot(x_dt.astype(BF16), m_t, preferred_element_type=F32))
            x_state.append((x_dt * to_end[k:k + 1, :]).astype(BF16))
        ylocal_s[:, cols] = jnp.concatenate(y_diag, axis=0) + pxs_ref[P_DSKIP] * xs
        newst_s[c] = jnp.dot(jnp.concatenate(x_state, axis=0), bt_s[c], preferred_element_type=F32)

    state_s[...] = jnp.zeros(state_s.shape, F32)
    for c in range(n_chunks):
        cols = slice(c * L, (c + 1) * L)
        out_decay = jnp.concatenate(
            [jnp.broadcast_to(out_decay_all[rows_of(c)][k:k + 1, :], (HEAD_DIM, L))
             for k in range(HEADS_PER_GROUP)], axis=0)
        carry_decay = jnp.concatenate(
            [jnp.broadcast_to(carry_decay_all[rows_of(c)][k:k + 1, :], (HEAD_DIM, L))
             for k in range(HEADS_PER_GROUP)], axis=0)
        prev = state_s[...]
        y_off = jnp.dot(prev.astype(BF16), c_ref[:, cols], preferred_element_type=F32) * out_decay
        state_s[...] = prev * carry_decay + newst_s[c]
        y = (ylocal_s[:, cols] + y_off) * zs_ref[:, cols].astype(F32)
        ms = jnp.mean(y * y, axis=0, keepdims=True)
        yn = (y * lax.rsqrt(ms + EPS)) * pxs_ref[P_NORMG]
        y_ref[cols, :] = yn.T.astype(BF16)


def _ssd(feat_t, dt_t, pxs, pdt, batch, seq):
    g_rows = GROUP_ROWS
    z_blk0 = F_Z // g_rows
    xs_blk0 = F_XS // g_rows
    b_blk0 = F_B // D_STATE
    c_blk0 = F_C // D_STATE
    return pl.pallas_call(
        functools.partial(_ssd_kernel, seq=seq),
        grid=(batch, N_SSM_GROUPS),
        in_specs=[
            pl.BlockSpec((g_rows, seq), lambda b, g: (z_blk0 + g, b)),
            pl.BlockSpec((g_rows, seq), lambda b, g: (xs_blk0 + g, b)),
            pl.BlockSpec((D_STATE, seq), lambda b, g: (b_blk0 + g, b)),
            pl.BlockSpec((D_STATE, seq), lambda b, g: (c_blk0 + g, b)),
            pl.BlockSpec((DT_ROWS_PER_GROUP, seq), lambda b, g: (g, b)),
            pl.BlockSpec((N_XS_PARAMS, g_rows, LANES), lambda b, g: (0, g, 0)),
            pl.BlockSpec((2, DT_ROWS_PER_GROUP, LANES), lambda b, g: (0, g, 0)),
        ],
        out_specs=pl.BlockSpec((seq, g_rows), lambda b, g: (b, g)),
        out_shape=jax.ShapeDtypeStruct((batch * seq, D_SSM), BF16),
        scratch_shapes=[
            pltpu.VMEM((seq // SSD_CHUNK, SSD_CHUNK, D_STATE), BF16),
            pltpu.VMEM((seq // SSD_CHUNK, SSD_CHUNK, SSD_CHUNK), F32),
            pltpu.VMEM((g_rows, seq), F32),
            pltpu.VMEM((seq // SSD_CHUNK, g_rows, D_STATE), F32),
            pltpu.VMEM((g_rows, D_STATE), F32),
        ],
        compiler_params=pltpu.CompilerParams(
            dimension_semantics=("arbitrary", "arbitrary"), vmem_limit_bytes=40 * MIB),
        name="ssd_mixer",
    )(feat_t, feat_t, feat_t, feat_t, dt_t, pxs, pdt)


OUTPROJ_TM = 1024


def _outproj_kernel(x_ref, a_ref, y_ref, wa_ref, wy_ref, o_ref):
    o_ref[...] = (x_ref[...]
                  + jnp.dot(a_ref[...], wa_ref[...], preferred_element_type=F32)
                  + jnp.dot(y_ref[...], wy_ref[...], preferred_element_type=F32))


def _outproj(x2, att, y, w_att, w_ssm):
    tokens = x2.shape[0]
    tm = OUTPROJ_TM
    resident = dict(pipeline_mode=pl.Buffered(1))
    return pl.pallas_call(
        _outproj_kernel,
        grid=(tokens // tm,),
        in_specs=[
            pl.BlockSpec((tm, D_MODEL), lambda i: (i, 0)),
            pl.BlockSpec((tm, D_ATT), lambda i: (i, 0)),
            pl.BlockSpec((tm, D_SSM), lambda i: (i, 0)),
            pl.BlockSpec((D_ATT, D_MODEL), lambda i: (0, 0), **resident),
            pl.BlockSpec((D_SSM, D_MODEL), lambda i: (0, 0), **resident),
        ],
        out_specs=pl.BlockSpec((tm, D_MODEL), lambda i: (i, 0)),
        out_shape=jax.ShapeDtypeStruct((tokens, D_MODEL), F32),
        compiler_params=pltpu.CompilerParams(
            dimension_semantics=("arbitrary",), vmem_limit_bytes=40 * MIB),
        name="outproj",
    )(x2, att, y, w_att, w_ssm)


FFN_TM = 512
FFN_TN = 256
FFN_HALO = BF16_SUBLANES
FFN_U_BUFFERS = 4


def _ffn_kernel(x_ref, g2_ref, wup_ref, cw_ref, cb_ref, wdn_ref, gf_ref, o_ref,
                h_s, u_s, act_s, *, tiles_per_seq):
    i = pl.program_id(0)
    tm = FFN_TM
    halo = FFN_HALO

    @pl.when(i % tiles_per_seq == 0)
    def _():
        h_s[0:halo, :] = jnp.zeros((halo, D_MODEL), BF16)

    x = x_ref[...]
    h_s[halo:, :] = _rms_rows(x, g2_ref[...]).astype(BF16)
    h_ext = h_s[...]

    def conv(col0, width, buf):
        u_s[buf] = jnp.dot(h_ext, wup_ref[:, col0:col0 + width], preferred_element_type=F32)
        acc = cb_ref[:, col0:col0 + width]
        for k in range(FFN_CONV):
            shift = FFN_CONV - 1 - k
            acc = acc + cw_ref[k:k + 1, col0:col0 + width] * u_s[buf, halo - shift:halo - shift + tm, :]
        return acc

    for c, n0 in enumerate(range(0, D_FF, FFN_TN)):
        gate = conv(n0, FFN_TN, (2 * c) % FFN_U_BUFFERS)
        val = conv(D_FF + n0, FFN_TN, (2 * c + 1) % FFN_U_BUFFERS)
        act_s[:, n0:n0 + FFN_TN] = ((gate * _sigmoid(gate)) * val).astype(BF16)

    h_s[0:halo, :] = h_s[tm:tm + halo, :]
    half = (D_FF // (2 * LANES)) * LANES
    x2 = (x + jnp.dot(act_s[:, :half], wdn_ref[:half, :], preferred_element_type=F32)
          + jnp.dot(act_s[:, half:], wdn_ref[half:, :], preferred_element_type=F32))
    o_ref[...] = _rms_rows(x2, gf_ref[...])


def _ffn(x1, ln2_g, w_up, conv_w, conv_b, w_down, lnf_g, seq):
    tokens = x1.shape[0]
    tm = FFN_TM
    resident = dict(pipeline_mode=pl.Buffered(1))
    return pl.pallas_call(
        functools.partial(_ffn_kernel, tiles_per_seq=seq // tm),
        grid=(tokens // tm,),
        in_specs=[
            pl.BlockSpec((tm, D_MODEL), lambda i: (i, 0)),
            pl.BlockSpec((1, D_MODEL), lambda i: (0, 0)),
            pl.BlockSpec((D_MODEL, 2 * D_FF), lambda i: (0, 0), **resident),
            pl.BlockSpec((FFN_CONV, 2 * D_FF), lambda i: (0, 0)),
            pl.BlockSpec((1, 2 * D_FF), lambda i: (0, 0)),
            pl.BlockSpec((D_FF, D_MODEL), lambda i: (0, 0), **resident),
            pl.BlockSpec((1, D_MODEL), lambda i: (0, 0)),
        ],
        out_specs=pl.BlockSpec((tm, D_MODEL), lambda i: (i, 0)),
        out_shape=jax.ShapeDtypeStruct((tokens, D_MODEL), F32),
        scratch_shapes=[
            pltpu.VMEM((FFN_HALO + tm, D_MODEL), BF16),
            pltpu.VMEM((FFN_U_BUFFERS, FFN_HALO + tm, FFN_TN), F32),
            pltpu.VMEM((tm, D_FF), BF16),
        ],
        compiler_params=pltpu.CompilerParams(
            dimension_semantics=("arbitrary",), vmem_limit_bytes=48 * MIB),
        name="conv_ffn",
    )(x1, ln2_g, w_up, conv_w, conv_b, w_down, lnf_g)


def _lane_bcast(v):
    return jnp.broadcast_to(v[..., None], v.shape + (LANES,)).astype(F32)


def _group_pad(v):
    v = v.reshape(N_SSM_GROUPS, HEADS_PER_GROUP)
    v = jnp.pad(v, ((0, 0), (0, DT_ROWS_PER_GROUP - HEADS_PER_GROUP)))
    return v.reshape(N_SSM_GROUPS * DT_ROWS_PER_GROUP)


def _layer(x2, batch, seq, ln1_g, w_in, ssm_conv_w, ssm_conv_b, dt_bias, a_log, d_skip,
           attn_norm_g, ssm_norm_g, w_out, ln2_g, w_up, ffn_conv_w, ffn_conv_b, w_down, out_g):
    c_q, c_k, c_v, c_z = 0, D_ATT, 2 * D_ATT, 3 * D_ATT
    c_xs = c_z + D_SSM
    c_b = c_xs + D_SSM
    c_c = c_b + N_SSM_GROUPS * D_STATE
    c_dt = c_c + N_SSM_GROUPS * D_STATE

    w_k = w_in[:, c_k:c_v].astype(BF16)
    w_f_t = jnp.concatenate(
        [w_in[:, c_z:c_xs], w_in[:, c_xs:c_b], w_in[:, c_q:c_k], w_in[:, c_v:c_z],
         w_in[:, c_b:c_c], w_in[:, c_c:c_dt]], axis=1).T.astype(BF16)
    w_dt_t = w_in[:, c_dt:].T.reshape(N_SSM_GROUPS, HEADS_PER_GROUP, D_MODEL)
    w_dt_t = jnp.pad(w_dt_t, ((0, 0), (0, DT_ROWS_PER_GROUP - HEADS_PER_GROUP), (0, 0)))
    w_dt_t = w_dt_t.reshape(N_SSM_GROUPS * DT_ROWS_PER_GROUP, D_MODEL).astype(BF16)

    conv_p = _lane_bcast(jnp.concatenate([ssm_conv_w, ssm_conv_b[None, :]], axis=0))
    k_rm, feat_t, dt_t = _inproj(x2, ln1_g[None, :], w_k, w_f_t, w_dt_t, conv_p, seq)

    att = _attention(feat_t, k_rm, attn_norm_g[None, :], batch, seq)

    pxs = _lane_bcast(jnp.stack([jnp.repeat(d_skip, HEAD_DIM), ssm_norm_g], axis=0))
    pdt = _lane_bcast(jnp.stack([_group_pad(dt_bias), _group_pad(a_log)], axis=0))
    y = _ssd(feat_t, dt_t, pxs, pdt, batch, seq)

    w_out_b = w_out.astype(BF16)
    x1 = _outproj(x2, att, y, w_out_b[:D_ATT], w_out_b[D_ATT:])

    return _ffn(x1, ln2_g[None, :], w_up.astype(BF16), ffn_conv_w, ffn_conv_b[None, :],
                w_down.astype(BF16), out_g[None, :], seq)


def kernel(x, ln1_g, w_in, ssm_conv_w, ssm_conv_b, dt_bias, a_log, d_skip, attn_norm_g,
           ssm_norm_g, w_out, ln2_g, w_up, ffn_conv_w, ffn_conv_b, w_down, lnf_g):
    batch, seq, d_model = x.shape
    depth = ln1_g.shape[0]
    assert d_model == D_MODEL and depth == 1
    assert seq % MOBA_BLOCK == 0 and seq % FFN_TM == 0 and seq % INPROJ_TM == 0
    assert (batch * seq) % OUTPROJ_TM == 0
    x2 = x.reshape(batch * seq, d_model)
    out = _layer(x2, batch, seq, ln1_g[0], w_in[0], ssm_conv_w[0], ssm_conv_b[0], dt_bias[0],
                 a_log[0], d_skip[0], attn_norm_g[0], ssm_norm_g[0], w_out[0], ln2_g[0],
                 w_up[0], ffn_conv_w[0], ffn_conv_b[0], w_down[0], lnf_g)
    return out.reshape(batch, seq, d_model)
```

```python
import functools

import jax
import jax.numpy as jnp
from jax import lax
from jax.experimental import pallas as pl
from jax.experimental.pallas import tpu as pltpu

F32 = jnp.float32
BF16 = jnp.bfloat16
HIGHEST = lax.Precision.HIGHEST

EPS = 1e-6
NEG_INF = -1e30
LOG2_E = 1.4426950408889634

D_MODEL = 1024
HEAD_DIM = 64
D_ATT = 512
N_ATT_HEADS = D_ATT // HEAD_DIM
MOBA_BLOCK = 256
MOBA_TOPK = 3
D_SSM = 1536
N_SSM_GROUPS = 4
HEADS_PER_GROUP = 6
GROUP_ROWS = HEADS_PER_GROUP * HEAD_DIM
D_STATE = 128
SSM_CONV = 4
SSD_CHUNK = 128
D_FF = 2816
FFN_CONV = 3

LANES = 128
SUBLANES = 8
BF16_SUBLANES = 16
DT_ROWS_PER_GROUP = SUBLANES

F_Z = 0
F_XS = F_Z + D_SSM
F_Q = F_XS + D_SSM
F_V = F_Q + D_ATT
F_B = F_V + D_ATT
F_C = F_B + N_SSM_GROUPS * D_STATE
N_FEAT = F_C + N_SSM_GROUPS * D_STATE

NT_DIMS = (((1,), (1,)), ((), ()))

MIB = 1024 * 1024


def _sigmoid(x):
    return 1.0 / (1.0 + jnp.exp(-x))


def _rms_rows(x, gain):
    ms = jnp.mean(x * x, axis=-1, keepdims=True)
    return (x * lax.rsqrt(ms + EPS)) * gain


INPROJ_TM = 512
INPROJ_TN = 256


def _conv_silu_time_on_lanes(cur, tail_rolled, cp_ref, ch0):
    rows, width = cur.shape
    lane = lax.broadcasted_iota(jnp.int32, (1, LANES), 1)
    taps = [cp_ref[k, ch0:ch0 + rows, :] for k in range(SSM_CONV)]
    bias = cp_ref[SSM_CONV, ch0:ch0 + rows, :]
    blocks = [cur[:, c * LANES:(c + 1) * LANES] for c in range(width // LANES)]
    rolled = [tail_rolled] + [[pltpu.roll(b, k, axis=1) for k in range(1, SSM_CONV)] for b in blocks]
    out = []
    for c, block in enumerate(blocks):
        acc = bias + taps[SSM_CONV - 1] * block
        for k in range(1, SSM_CONV):
            shifted = jnp.where(lane >= k, rolled[c + 1][k - 1], rolled[c][k - 1])
            acc = acc + taps[SSM_CONV - 1 - k] * shifted
        out.append(acc * _sigmoid(acc))
    return jnp.concatenate(out, axis=1), rolled[-1]


def _inproj_kernel(x_ref, g_ref, wk_ref, wf_ref, wdt_ref, cp_ref, k_ref, f_ref, dt_ref, tail_s,
                   *, tiles_per_seq):
    tn = INPROJ_TN
    tm = x_ref.shape[0]

    @pl.when(pl.program_id(0) % tiles_per_seq == 0)
    def _():
        tail_s[...] = jnp.zeros(tail_s.shape, F32)

    h = _rms_rows(x_ref[...], g_ref[...]).astype(BF16)
    order = [n0 for n0 in range(0, N_FEAT, tn) if not F_Q <= n0 < F_B]
    order += [n0 for n0 in range(0, N_FEAT, tn) if F_Q <= n0 < F_B]
    for n0 in order:
        res = lax.dot_general(wf_ref[n0:n0 + tn, :], h, NT_DIMS, preferred_element_type=F32)
        if n0 < F_XS:
            res = res * _sigmoid(res)
        elif n0 < F_Q or n0 >= F_B:
            ch0 = n0 - F_XS if n0 < F_Q else n0 - F_B + D_SSM
            tail_rolled = [tail_s[k, ch0:ch0 + tn, :] for k in range(SSM_CONV - 1)]
            res, new_tail = _conv_silu_time_on_lanes(res, tail_rolled, cp_ref, ch0)
            for k in range(SSM_CONV - 1):
                tail_s[k, ch0:ch0 + tn, :] = new_tail[k]
        f_ref[n0:n0 + tn, :] = res.astype(BF16)
    k_ref[...] = jnp.dot(h, wk_ref[...], preferred_element_type=F32).astype(BF16)
    dt_ref[...] = lax.dot_general(wdt_ref[...], h, NT_DIMS, preferred_element_type=F32)


def _inproj(x2, ln_g, w_k, w_f_t, w_dt_t, conv_p, seq):
    tokens = x2.shape[0]
    tm = INPROJ_TM
    dt_rows = w_dt_t.shape[0]
    d_xbc = conv_p.shape[1]
    resident = dict(pipeline_mode=pl.Buffered(1))
    return pl.pallas_call(
        functools.partial(_inproj_kernel, tiles_per_seq=seq // tm),
        grid=(tokens // tm,),
        in_specs=[
            pl.BlockSpec((tm, D_MODEL), lambda i: (i, 0)),
            pl.BlockSpec((1, D_MODEL), lambda i: (0, 0)),
            pl.BlockSpec((D_MODEL, D_ATT), lambda i: (0, 0), **resident),
            pl.BlockSpec((N_FEAT, D_MODEL), lambda i: (0, 0), **resident),
            pl.BlockSpec((dt_rows, D_MODEL), lambda i: (0, 0), **resident),
            pl.BlockSpec((SSM_CONV + 1, d_xbc, LANES), lambda i: (0, 0, 0), **resident),
        ],
        out_specs=[
            pl.BlockSpec((tm, D_ATT), lambda i: (i, 0)),
            pl.BlockSpec((N_FEAT, tm), lambda i: (0, i)),
            pl.BlockSpec((dt_rows, tm), lambda i: (0, i)),
        ],
        out_shape=[
            jax.ShapeDtypeStruct((tokens, D_ATT), BF16),
            jax.ShapeDtypeStruct((N_FEAT, tokens), BF16),
            jax.ShapeDtypeStruct((dt_rows, tokens), F32),
        ],
        scratch_shapes=[pltpu.VMEM((SSM_CONV - 1, d_xbc, LANES), F32)],
        compiler_params=pltpu.CompilerParams(
            dimension_semantics=("arbitrary",), vmem_limit_bytes=52 * MIB),
        name="inproj",
    )(x2, ln_g, w_k, w_f_t, w_dt_t, conv_p)


ATTN_SCORE_LEAD = 2


def _alibi_slope(head):
    return float(2.0 ** (-8.0 * (head + 1) / N_ATT_HEADS))


def _attn_kernel(q_ref, k_ref, kown_ref, v_ref, vown_ref, g_ref, o_ref,
                 kmean_s, qm_s, sel_s, alibi_s, t_s, tmax_s, m_s, acc_s, out_s, *, nb):
    i = pl.program_id(1)
    blk = MOBA_BLOCK
    pair = 2 * HEAD_DIM

    @pl.when(i == 0)
    def _():
        kmean_s[...] = jnp.zeros(kmean_s.shape, F32)
        for j in range(nb):
            kb = k_ref[j * blk:(j + 1) * blk, :].astype(F32)
            kmean_s[j:j + 1, :] = jnp.mean(kb, axis=0, keepdims=True)
        key_off = lax.broadcasted_iota(jnp.int32, (blk, blk), 0).astype(F32)
        for head in range(N_ATT_HEADS):
            alibi_s[head] = (_alibi_slope(head) * LOG2_E) * key_off

    cand = lax.broadcasted_iota(jnp.int32, (kmean_s.shape[0], blk), 0)
    valid = cand < i
    feat = lax.broadcasted_iota(jnp.int32, (pair, blk), 0)

    pair_lane = lax.broadcasted_iota(jnp.int32, (kmean_s.shape[0], pair), 1)
    for hp in range(N_ATT_HEADS // 2):
        q_raw = q_ref[hp * pair:(hp + 1) * pair, :]
        q2 = q_raw.astype(F32) * (HEAD_DIM ** -0.5 * LOG2_E)
        kmean_pair = kmean_s[:, hp * pair:(hp + 1) * pair]
        for sub in range(2):
            head = 2 * hp + sub
            in_head = (feat < HEAD_DIM) if sub == 0 else (feat >= HEAD_DIM)
            qm_s[head] = jnp.where(in_head, q2, 0.0).astype(BF16)

            km = jnp.where((pair_lane < HEAD_DIM) if sub == 0 else (pair_lane >= HEAD_DIM), kmean_pair, 0.0)
            km_hi = km.astype(BF16)
            rest = km - km_hi.astype(F32)
            km_mid = rest.astype(BF16)
            km_lo = (rest - km_mid.astype(F32)).astype(BF16)
            gate = (jnp.dot(km_lo, q_raw, preferred_element_type=F32)
                    + jnp.dot(km_mid, q_raw, preferred_element_type=F32)
                    + jnp.dot(km_hi, q_raw, preferred_element_type=F32))
            gate = jnp.where(valid, gate, NEG_INF)
            rank = jnp.zeros(gate.shape, F32)
            for jp in range(nb):
                gj = gate[jp:jp + 1, :]
                ge = jnp.where(gj >= gate, 1.0, 0.0)
                gt = jnp.where(gj > gate, 1.0, 0.0)
                rank = rank + jnp.where(cand > jp, ge, gt)
            sel_s[head] = jnp.where(valid, jnp.where(rank < MOBA_TOPK, 0.0, NEG_INF), NEG_INF)

    m_s[...] = jnp.full(m_s.shape, NEG_INF, F32)
    acc_s[...] = jnp.zeros(acc_s.shape, F32)

    causal = (lax.broadcasted_iota(jnp.int32, (blk, blk), 0)
              <= lax.broadcasted_iota(jnp.int32, (blk, blk), 1))

    all_heads = tuple(range(N_ATT_HEADS))

    def scores(tile, slot, heads=all_heads):
        for head in heads:
            cols = slice((head // 2) * pair, (head // 2 + 1) * pair)
            kj = kown_ref[:, cols] if tile is None else k_ref[tile * blk:(tile + 1) * blk, cols]
            t = jnp.dot(kj, qm_s[head], preferred_element_type=F32) + alibi_s[head]
            if tile is None:
                t = jnp.where(causal, t, NEG_INF)
            t_s[slot, head] = t
            quarter = blk // 4
            tq = jnp.maximum(jnp.maximum(t[0:quarter], t[quarter:2 * quarter]),
                             jnp.maximum(t[2 * quarter:3 * quarter], t[3 * quarter:]))
            tmax_s[slot, head, 0:1, :] = jnp.max(tq, axis=0, keepdims=True)

    ones_rows = jnp.ones((BF16_SUBLANES, blk), BF16)

    def accumulate(tile, slot, heads=all_heads):
        for head in heads:
            rows = slice(head * HEAD_DIM, (head + 1) * HEAD_DIM)
            m_old = m_s[head, 0:1, :]
            if tile is None:
                m_new = jnp.maximum(m_old, tmax_s[slot, head, 0:1, :])
                shift = m_new
                v = vown_ref[rows, :]
            else:
                row = (sel_s[head, tile:tile + 1, :]
                       + (_alibi_slope(head) * LOG2_E * blk) * (tile - i).astype(F32))
                m_new = jnp.maximum(m_old, tmax_s[slot, head, 0:1, :] + row)
                shift = m_new - row
                v = v_ref[rows, tile * blk:(tile + 1) * blk]
            p = jnp.exp2(t_s[slot, head] - shift)
            alpha = jnp.exp2(m_old - m_new)
            m_s[head, 0:1, :] = m_new
            vj = jnp.concatenate([v, ones_rows], axis=0)
            acc_s[head] = alpha * acc_s[head] + jnp.dot(vj, p.astype(BF16), preferred_element_type=F32)

    scores(None, 0)
    for j in range(nb - 1):
        @pl.when(j < i)
        def _(j=j):
            prev = None if j == 0 else j - 1
            scores(j, (j + 1) % 2, all_heads[:ATTN_SCORE_LEAD])
            for head in all_heads:
                accumulate(prev, j % 2, (head,))
                scores(j, (j + 1) % 2, all_heads[head + ATTN_SCORE_LEAD:head + ATTN_SCORE_LEAD + 1])

    @pl.when(i == 0)
    def _():
        accumulate(None, 0)

    for j in range(nb - 1):
        @pl.when(i == j + 1)
        def _(j=j):
            accumulate(j, (j + 1) % 2)

    for head in range(N_ATT_HEADS):
        out_s[head * HEAD_DIM:(head + 1) * HEAD_DIM, :] = (
            acc_s[head, 0:HEAD_DIM, :] / acc_s[head, HEAD_DIM:HEAD_DIM + 1, :])
    att = out_s[...].T
    o_ref[...] = _rms_rows(att, g_ref[...]).astype(BF16)


def _attention(feat_t, k_rm, attn_g, batch, seq):
    nb = seq // MOBA_BLOCK
    cands = max(SUBLANES, nb)
    q_blk = F_Q // D_ATT
    v_blk = F_V // D_ATT
    return pl.pallas_call(
        functools.partial(_attn_kernel, nb=nb),
        grid=(batch, nb),
        in_specs=[
            pl.BlockSpec((D_ATT, MOBA_BLOCK), lambda b, i: (q_blk, b * nb + i)),
            pl.BlockSpec((seq, D_ATT), lambda b, i: (b, 0)),
            pl.BlockSpec((MOBA_BLOCK, D_ATT), lambda b, i: (b * nb + i, 0)),
            pl.BlockSpec((D_ATT, seq), lambda b, i: (v_blk, b)),
            pl.BlockSpec((D_ATT, MOBA_BLOCK), lambda b, i: (v_blk, b * nb + i)),
            pl.BlockSpec((1, D_ATT), lambda b, i: (0, 0)),
        ],
        out_specs=pl.BlockSpec((MOBA_BLOCK, D_ATT), lambda b, i: (b * nb + i, 0)),
        out_shape=jax.ShapeDtypeStruct((batch * seq, D_ATT), BF16),
        scratch_shapes=[
            pltpu.VMEM((cands, D_ATT), F32),
            pltpu.VMEM((N_ATT_HEADS, 2 * HEAD_DIM, MOBA_BLOCK), BF16),
            pltpu.VMEM((N_ATT_HEADS, cands, MOBA_BLOCK), F32),
            pltpu.VMEM((N_ATT_HEADS, MOBA_BLOCK, MOBA_BLOCK), F32),
            pltpu.VMEM((2, N_ATT_HEADS, MOBA_BLOCK, MOBA_BLOCK), F32),
            pltpu.VMEM((2, N_ATT_HEADS, SUBLANES, MOBA_BLOCK), F32),
            pltpu.VMEM((N_ATT_HEADS, SUBLANES, MOBA_BLOCK), F32),
            pltpu.VMEM((N_ATT_HEADS, HEAD_DIM + BF16_SUBLANES, MOBA_BLOCK), F32),
            pltpu.VMEM((D_ATT, MOBA_BLOCK), F32),
        ],
        compiler_params=pltpu.CompilerParams(
            dimension_semantics=("arbitrary", "arbitrary"), vmem_limit_bytes=32 * MIB),
        name="moba_attention",
    )(feat_t, k_rm, k_rm, feat_t, feat_t, attn_g)


P_DSKIP, P_NORMG = 0, 1
N_XS_PARAMS = 2


def _ssd_kernel(zs_ref, xs_ref, b_ref, c_ref, dt_ref, pxs_ref, pdt_ref, y_ref,
                bt_s, cbt_s, ylocal_s, newst_s, state_s, *, seq):
    L = SSD_CHUNK
    n_chunks = seq // L
    r_i = lax.broadcasted_iota(jnp.int32, (L, L), 0)
    c_i = lax.broadcasted_iota(jnp.int32, (L, L), 1)
    upper = jnp.where(r_i <= c_i, 1.0, 0.0).astype(F32)
    s_le_l = r_i <= c_i

    dt_bias = pdt_ref[0]
    a_neg = -jnp.exp(pdt_ref[1])

    raw = jnp.concatenate([dt_ref[:, c * L:(c + 1) * L] + dt_bias for c in range(n_chunks)], axis=0)
    dt_all = jnp.maximum(raw, 0.0) + jnp.log1p(jnp.exp(-jnp.abs(raw)))
    a_dt = dt_all * jnp.concatenate([a_neg] * n_chunks, axis=0)
    acs_all = jnp.dot(a_dt, upper, precision=HIGHEST, preferred_element_type=F32)
    acs_log2 = acs_all * LOG2_E
    acs_log2_t = acs_log2.T
    acs_end = jnp.broadcast_to(acs_all[:, L - 1:L], acs_all.shape)
    to_end_all = jnp.exp(acs_end - acs_all)
    out_decay_all = jnp.exp(acs_all)
    carry_decay_all = jnp.exp(acs_end)

    def rows_of(c):
        return slice(c * DT_ROWS_PER_GROUP, (c + 1) * DT_ROWS_PER_GROUP)

    def head_rows(k):
        return slice(k * HEAD_DIM, (k + 1) * HEAD_DIM)

    for c in range(n_chunks):
        b_t = b_ref[:, c * L:(c + 1) * L].astype(F32).T.astype(BF16)
        bt_s[c] = b_t
        cbt_s[c] = jnp.dot(b_t, c_ref[:, c * L:(c + 1) * L], preferred_element_type=F32)

    for c in range(n_chunks):
        cols = slice(c * L, (c + 1) * L)
        xs = xs_ref[:, cols].astype(F32)
        dt = dt_all[rows_of(c)]
        acs_row = acs_log2[rows_of(c)]
        acs_col = acs_log2_t[:, rows_of(c)]
        to_end = to_end_all[rows_of(c)]
        cb_t = cbt_s[c]
        y_diag, x_state = [], []
        for k in range(HEADS_PER_GROUP):
            x_dt = xs[head_rows(k), :] * dt[k:k + 1, :]
            seg = acs_row[k:k + 1, :] - acs_col[:, k:k + 1]
            decay = jnp.exp2(jnp.where(s_le_l, seg, NEG_INF))
            m_t = (cb_t * decay).astype(BF16)
            y_diag.append(jnp.dot(x_dt.astype(BF16), m_t, preferred_element_type=F32))
            x_state.append((x_dt * to_end[k:k + 1, :]).astype(BF16))
        ylocal_s[:, cols] = jnp.concatenate(y_diag, axis=0) + pxs_ref[P_DSKIP] * xs
        newst_s[c] = jnp.dot(jnp.concatenate(x_state, axis=0), bt_s[c], preferred_element_type=F32)

    state_s[...] = jnp.zeros(state_s.shape, F32)
    for c in range(n_chunks):
        cols = slice(c * L, (c + 1) * L)
        out_decay = jnp.concatenate(
            [jnp.broadcast_to(out_decay_all[rows_of(c)][k:k + 1, :], (HEAD_DIM, L))
             for k in range(HEADS_PER_GROUP)], axis=0)
        carry_decay = jnp.concatenate(
            [jnp.broadcast_to(carry_decay_all[rows_of(c)][k:k + 1, :], (HEAD_DIM, L))
             for k in range(HEADS_PER_GROUP)], axis=0)
        prev = state_s[...]
        y_off = jnp.dot(prev.astype(BF16), c_ref[:, cols], preferred_element_type=F32) * out_decay
        state_s[...] = prev * carry_decay + newst_s[c]
        y = (ylocal_s[:, cols] + y_off) * zs_ref[:, cols].astype(F32)
        ms = jnp.mean(y * y, axis=0, keepdims=True)
        yn = (y * lax.rsqrt(ms + EPS)) * pxs_ref[P_NORMG]
        y_ref[cols, :] = yn.T.astype(BF16)


def _ssd(feat_t, dt_t, pxs, pdt, batch, seq):
    g_rows = GROUP_ROWS
    z_blk0 = F_Z // g_rows
    xs_blk0 = F_XS // g_rows
    b_blk0 = F_B // D_STATE
    c_blk0 = F_C // D_STATE
    return pl.pallas_call(
        functools.partial(_ssd_kernel, seq=seq),
        grid=(batch, N_SSM_GROUPS),
        in_specs=[
            pl.BlockSpec((g_rows, seq), lambda b, g: (z_blk0 + g, b)),
            pl.BlockSpec((g_rows, seq), lambda b, g: (xs_blk0 + g, b)),
            pl.BlockSpec((D_STATE, seq), lambda b, g: (b_blk0 + g, b)),
            pl.BlockSpec((D_STATE, seq), lambda b, g: (c_blk0 + g, b)),
            pl.BlockSpec((DT_ROWS_PER_GROUP, seq), lambda b, g: (g, b)),
            pl.BlockSpec((N_XS_PARAMS, g_rows, LANES), lambda b, g: (0, g, 0)),
            pl.BlockSpec((2, DT_ROWS_PER_GROUP, LANES), lambda b, g: (0, g, 0)),
        ],
        out_specs=pl.BlockSpec((seq, g_rows), lambda b, g: (b, g)),
        out_shape=jax.ShapeDtypeStruct((batch * seq, D_SSM), BF16),
        scratch_shapes=[
            pltpu.VMEM((seq // SSD_CHUNK, SSD_CHUNK, D_STATE), BF16),
            pltpu.VMEM((seq // SSD_CHUNK, SSD_CHUNK, SSD_CHUNK), F32),
            pltpu.VMEM((g_rows, seq), F32),
            pltpu.VMEM((seq // SSD_CHUNK, g_rows, D_STATE), F32),
            pltpu.VMEM((g_rows, D_STATE), F32),
        ],
        compiler_params=pltpu.CompilerParams(
            dimension_semantics=("arbitrary", "arbitrary"), vmem_limit_bytes=40 * MIB),
        name="ssd_mixer",
    )(feat_t, feat_t, feat_t, feat_t, dt_t, pxs, pdt)


OUTPROJ_TM = 1024


def _outproj_kernel(x_ref, a_ref, y_ref, wa_ref, wy_ref, o_ref):
    o_ref[...] = (x_ref[...]
                  + jnp.dot(a_ref[...], wa_ref[...], preferred_element_type=F32)
                  + jnp.dot(y_ref[...], wy_ref[...], preferred_element_type=F32))


def _outproj(x2, att, y, w_att, w_ssm):
    tokens = x2.shape[0]
    tm = OUTPROJ_TM
    resident = dict(pipeline_mode=pl.Buffered(1))
    return pl.pallas_call(
        _outproj_kernel,
        grid=(tokens // tm,),
        in_specs=[
            pl.BlockSpec((tm, D_MODEL), lambda i: (i, 0)),
            pl.BlockSpec((tm, D_ATT), lambda i: (i, 0)),
            pl.BlockSpec((tm, D_SSM), lambda i: (i, 0)),
            pl.BlockSpec((D_ATT, D_MODEL), lambda i: (0, 0), **resident),
            pl.BlockSpec((D_SSM, D_MODEL), lambda i: (0, 0), **resident),
        ],
        out_specs=pl.BlockSpec((tm, D_MODEL), lambda i: (i, 0)),
        out_shape=jax.ShapeDtypeStruct((tokens, D_MODEL), F32),
        compiler_params=pltpu.CompilerParams(
            dimension_semantics=("arbitrary",), vmem_limit_bytes=40 * MIB),
        name="outproj",
    )(x2, att, y, w_att, w_ssm)


FFN_TM = 512
FFN_TN = 256
FFN_HALO = BF16_SUBLANES
FFN_U_BUFFERS = 2 * (D_FF // FFN_TN)


def _ffn_kernel(x_ref, g2_ref, wup_ref, cw_ref, cb_ref, wdn_ref, gf_ref, o_ref,
                h_s, u_s, act_s, *, tiles_per_seq):
    i = pl.program_id(0)
    tm = FFN_TM
    halo = FFN_HALO

    @pl.when(i % tiles_per_seq == 0)
    def _():
        h_s[0:halo, :] = jnp.zeros((halo, D_MODEL), BF16)

    x = x_ref[...]
    h_s[halo:, :] = _rms_rows(x, g2_ref[...]).astype(BF16)
    h_ext = h_s[...]

    def conv(col0, width, buf):
        u_s[buf] = jnp.dot(h_ext, wup_ref[:, col0:col0 + width], preferred_element_type=F32)
        acc = cb_ref[:, col0:col0 + width]
        for k in range(FFN_CONV):
            shift = FFN_CONV - 1 - k
            acc = acc + cw_ref[k:k + 1, col0:col0 + width] * u_s[buf, halo - shift:halo - shift + tm, :]
        return acc

    for c, n0 in enumerate(range(0, D_FF, FFN_TN)):
        gate = conv(n0, FFN_TN, (2 * c) % FFN_U_BUFFERS)
        val = conv(D_FF + n0, FFN_TN, (2 * c + 1) % FFN_U_BUFFERS)
        act_s[:, n0:n0 + FFN_TN] = ((gate * _sigmoid(gate)) * val).astype(BF16)

    h_s[0:halo, :] = h_s[tm:tm + halo, :]
    half = (D_FF // (2 * LANES)) * LANES
    x2 = (x + jnp.dot(act_s[:, :half], wdn_ref[:half, :], preferred_element_type=F32)
          + jnp.dot(act_s[:, half:], wdn_ref[half:, :], preferred_element_type=F32))
    o_ref[...] = _rms_rows(x2, gf_ref[...])


def _ffn(x1, ln2_g, w_up, conv_w, conv_b, w_down, lnf_g, seq):
    tokens = x1.shape[0]
    tm = FFN_TM
    resident = dict(pipeline_mode=pl.Buffered(1))
    return pl.pallas_call(
        functools.partial(_ffn_kernel, tiles_per_seq=seq // tm),
        grid=(tokens // tm,),
        in_specs=[
            pl.BlockSpec((tm, D_MODEL), lambda i: (i, 0)),
            pl.BlockSpec((1, D_MODEL), lambda i: (0, 0)),
            pl.BlockSpec((D_MODEL, 2 * D_FF), lambda i: (0, 0), **resident),
            pl.BlockSpec((FFN_CONV, 2 * D_FF), lambda i: (0, 0)),
            pl.BlockSpec((1, 2 * D_FF), lambda i: (0, 0)),
            pl.BlockSpec((D_FF, D_MODEL), lambda i: (0, 0), **resident),
            pl.BlockSpec((1, D_MODEL), lambda i: (0, 0)),
        ],
        out_specs=pl.BlockSpec((tm, D_MODEL), lambda i: (i, 0)),
        out_shape=jax.ShapeDtypeStruct((tokens, D_MODEL), F32),
        scratch_shapes=[
            pltpu.VMEM((FFN_HALO + tm, D_MODEL), BF16),
            pltpu.VMEM((FFN_U_BUFFERS, FFN_HALO + tm, FFN_TN), F32),
            pltpu.VMEM((tm, D_FF), BF16),
        ],
        compiler_params=pltpu.CompilerParams(
            dimension_semantics=("arbitrary",), vmem_limit_bytes=48 * MIB),
        name="conv_ffn",
    )(x1, ln2_g, w_up, conv_w, conv_b, w_down, lnf_g)


def _lane_bcast(v):
    return jnp.broadcast_to(v[..., None], v.shape + (LANES,)).astype(F32)


def _group_pad(v):
    v = v.reshape(N_SSM_GROUPS, HEADS_PER_GROUP)
    v = jnp.pad(v, ((0, 0), (0, DT_ROWS_PER_GROUP - HEADS_PER_GROUP)))
    return v.reshape(N_SSM_GROUPS * DT_ROWS_PER_GROUP)


def _layer(x2, batch, seq, ln1_g, w_in, ssm_conv_w, ssm_conv_b, dt_bias, a_log, d_skip,
           attn_norm_g, ssm_norm_g, w_out, ln2_g, w_up, ffn_conv_w, ffn_conv_b, w_down, out_g):
    c_q, c_k, c_v, c_z = 0, D_ATT, 2 * D_ATT, 3 * D_ATT
    c_xs = c_z + D_SSM
    c_b = c_xs + D_SSM
    c_c = c_b + N_SSM_GROUPS * D_STATE
    c_dt = c_c + N_SSM_GROUPS * D_STATE

    w_k = w_in[:, c_k:c_v].astype(BF16)
    w_f_t = jnp.concatenate(
        [w_in[:, c_z:c_xs], w_in[:, c_xs:c_b], w_in[:, c_q:c_k], w_in[:, c_v:c_z],
         w_in[:, c_b:c_c], w_in[:, c_c:c_dt]], axis=1).T.astype(BF16)
    w_dt_t = w_in[:, c_dt:].T.reshape(N_SSM_GROUPS, HEADS_PER_GROUP, D_MODEL)
    w_dt_t = jnp.pad(w_dt_t, ((0, 0), (0, DT_ROWS_PER_GROUP - HEADS_PER_GROUP), (0, 0)))
    w_dt_t = w_dt_t.reshape(N_SSM_GROUPS * DT_ROWS_PER_GROUP, D_MODEL).astype(BF16)

    conv_p = _lane_bcast(jnp.concatenate([ssm_conv_w, ssm_conv_b[None, :]], axis=0))
    k_rm, feat_t, dt_t = _inproj(x2, ln1_g[None, :], w_k, w_f_t, w_dt_t, conv_p, seq)

    att = _attention(feat_t, k_rm, attn_norm_g[None, :], batch, seq)

    pxs = _lane_bcast(jnp.stack([jnp.repeat(d_skip, HEAD_DIM), ssm_norm_g], axis=0))
    pdt = _lane_bcast(jnp.stack([_group_pad(dt_bias), _group_pad(a_log)], axis=0))
    y = _ssd(feat_t, dt_t, pxs, pdt, batch, seq)

    w_out_b = w_out.astype(BF16)
    x1 = _outproj(x2, att, y, w_out_b[:D_ATT], w_out_b[D_ATT:])

    return _ffn(x1, ln2_g[None, :], w_up.astype(BF16), ffn_conv_w, ffn_conv_b[None, :],
                w_down.astype(BF16), out_g[None, :], seq)


def kernel(x, ln1_g, w_in, ssm_conv_w, ssm_conv_b, dt_bias, a_log, d_skip, attn_norm_g,
           ssm_norm_g, w_out, ln2_g, w_up, ffn_conv_w, ffn_conv_b, w_down, lnf_g):
    batch, seq, d_model = x.shape
    depth = ln1_g.shape[0]
    assert d_model == D_MODEL and depth == 1
    assert seq % MOBA_BLOCK == 0 and seq % FFN_TM == 0 and seq % INPROJ_TM == 0
    assert (batch * seq) % OUTPROJ_TM == 0
    x2 = x.reshape(batch * seq, d_model)
    out = _layer(x2, batch, seq, ln1_g[0], w_in[0], ssm_conv_w[0], ssm_conv_b[0], dt_bias[0],
                 a_log[0], d_skip[0], attn_norm_g[0], ssm_norm_g[0], w_out[0], ln2_g[0],
                 w_up[0], ffn_conv_w[0], ffn_conv_b[0], w_down[0], lnf_g)
    return out.reshape(batch, seq, d_model)
```

```python
import functools

import jax
import jax.numpy as jnp
from jax import lax
from jax.experimental import pallas as pl
from jax.experimental.pallas import tpu as pltpu

F32 = jnp.float32
BF16 = jnp.bfloat16
HIGHEST = lax.Precision.HIGHEST

EPS = 1e-6
NEG_INF = -1e30
LOG2_E = 1.4426950408889634

D_MODEL = 1024
HEAD_DIM = 64
D_ATT = 512
N_ATT_HEADS = D_ATT // HEAD_DIM
MOBA_BLOCK = 256
MOBA_TOPK = 3
D_SSM = 1536
N_SSM_GROUPS = 4
HEADS_PER_GROUP = 6
GROUP_ROWS = HEADS_PER_GROUP * HEAD_DIM
D_STATE = 128
SSM_CONV = 4
SSD_CHUNK = 128
D_FF = 2816
FFN_CONV = 3

LANES = 128
SUBLANES = 8
BF16_SUBLANES = 16
DT_ROWS_PER_GROUP = SUBLANES

F_Z = 0
F_XS = F_Z + D_SSM
F_Q = F_XS + D_SSM
F_V = F_Q + D_ATT
F_B = F_V + D_ATT
F_C = F_B + N_SSM_GROUPS * D_STATE
N_FEAT = F_C + N_SSM_GROUPS * D_STATE

NT_DIMS = (((1,), (1,)), ((), ()))

MIB = 1024 * 1024


def _sigmoid(x):
    return 1.0 / (1.0 + jnp.exp(-x))


def _rms_rows(x, gain):
    ms = jnp.mean(x * x, axis=-1, keepdims=True)
    return (x * lax.rsqrt(ms + EPS)) * gain


INPROJ_TM = 512
INPROJ_TN = 256


def _conv_silu_time_on_lanes(cur, tail_rolled, cp_ref, ch0):
    rows, width = cur.shape
    lane = lax.broadcasted_iota(jnp.int32, (1, LANES), 1)
    taps = [cp_ref[k, ch0:ch0 + rows, :] for k in range(SSM_CONV)]
    bias = cp_ref[SSM_CONV, ch0:ch0 + rows, :]
    blocks = [cur[:, c * LANES:(c + 1) * LANES] for c in range(width // LANES)]
    rolled = [tail_rolled] + [[pltpu.roll(b, k, axis=1) for k in range(1, SSM_CONV)] for b in blocks]
    out = []
    for c, block in enumerate(blocks):
        acc = bias + taps[SSM_CONV - 1] * block
        for k in range(1, SSM_CONV):
            shifted = jnp.where(lane >= k, rolled[c + 1][k - 1], rolled[c][k - 1])
            acc = acc + taps[SSM_CONV - 1 - k] * shifted
        out.append(acc * _sigmoid(acc))
    return jnp.concatenate(out, axis=1), rolled[-1]


def _inproj_kernel(x_ref, g_ref, wk_ref, wf_ref, wdt_ref, cp_ref, k_ref, f_ref, dt_ref, tail_s,
                   *, tiles_per_seq):
    tn = INPROJ_TN
    tm = x_ref.shape[0]

    @pl.when(pl.program_id(0) % tiles_per_seq == 0)
    def _():
        tail_s[...] = jnp.zeros(tail_s.shape, F32)

    h = _rms_rows(x_ref[...], g_ref[...]).astype(BF16)
    order = [n0 for n0 in range(0, N_FEAT, tn) if not F_Q <= n0 < F_B]
    order += [n0 for n0 in range(0, N_FEAT, tn) if F_Q <= n0 < F_B]
    for n0 in order:
        res = lax.dot_general(wf_ref[n0:n0 + tn, :], h, NT_DIMS, preferred_element_type=F32)
        if n0 < F_XS:
            res = res * _sigmoid(res)
        elif n0 < F_Q or n0 >= F_B:
            ch0 = n0 - F_XS if n0 < F_Q else n0 - F_B + D_SSM
            tail_rolled = [tail_s[k, ch0:ch0 + tn, :] for k in range(SSM_CONV - 1)]
            res, new_tail = _conv_silu_time_on_lanes(res, tail_rolled, cp_ref, ch0)
            for k in range(SSM_CONV - 1):
                tail_s[k, ch0:ch0 + tn, :] = new_tail[k]
        f_ref[n0:n0 + tn, :] = res.astype(BF16)
    k_ref[...] = jnp.dot(h, wk_ref[...], preferred_element_type=F32).astype(BF16)
    dt_ref[...] = lax.dot_general(wdt_ref[...], h, NT_DIMS, preferred_element_type=F32)


def _inproj(x2, ln_g, w_k, w_f_t, w_dt_t, conv_p, seq):
    tokens = x2.shape[0]
    tm = INPROJ_TM
    dt_rows = w_dt_t.shape[0]
    d_xbc = conv_p.shape[1]
    resident = dict(pipeline_mode=pl.Buffered(1))
    return pl.pallas_call(
        functools.partial(_inproj_kernel, tiles_per_seq=seq // tm),
        grid=(tokens // tm,),
        in_specs=[
            pl.BlockSpec((tm, D_MODEL), lambda i: (i, 0)),
            pl.BlockSpec((1, D_MODEL), lambda i: (0, 0)),
            pl.BlockSpec((D_MODEL, D_ATT), lambda i: (0, 0), **resident),
            pl.BlockSpec((N_FEAT, D_MODEL), lambda i: (0, 0), **resident),
            pl.BlockSpec((dt_rows, D_MODEL), lambda i: (0, 0), **resident),
            pl.BlockSpec((SSM_CONV + 1, d_xbc, LANES), lambda i: (0, 0, 0), **resident),
        ],
        out_specs=[
            pl.BlockSpec((tm, D_ATT), lambda i: (i, 0)),
            pl.BlockSpec((N_FEAT, tm), lambda i: (0, i)),
            pl.BlockSpec((dt_rows, tm), lambda i: (0, i)),
        ],
        out_shape=[
            jax.ShapeDtypeStruct((tokens, D_ATT), BF16),
            jax.ShapeDtypeStruct((N_FEAT, tokens), BF16),
            jax.ShapeDtypeStruct((dt_rows, tokens), F32),
        ],
        scratch_shapes=[pltpu.VMEM((SSM_CONV - 1, d_xbc, LANES), F32)],
        compiler_params=pltpu.CompilerParams(
            dimension_semantics=("arbitrary",), vmem_limit_bytes=52 * MIB),
        name="inproj",
    )(x2, ln_g, w_k, w_f_t, w_dt_t, conv_p)


ATTN_SCORE_LEAD = 2


def _alibi_slope(head):
    return float(2.0 ** (-8.0 * (head + 1) / N_ATT_HEADS))


def _attn_kernel(q_ref, k_ref, kown_ref, v_ref, vown_ref, g_ref, o_ref,
                 kmean_s, qm_s, sel_s, alibi_s, t_s, tmax_s, m_s, acc_s, out_s, *, nb):
    i = pl.program_id(1)
    blk = MOBA_BLOCK
    pair = 2 * HEAD_DIM

    @pl.when(i == 0)
    def _():
        kmean_s[...] = jnp.zeros(kmean_s.shape, F32)
        for j in range(nb):
            kb = k_ref[j * blk:(j + 1) * blk, :].astype(F32)
            kmean_s[j:j + 1, :] = jnp.mean(kb, axis=0, keepdims=True)
        key_off = lax.broadcasted_iota(jnp.int32, (blk, blk), 0).astype(F32)
        for head in range(N_ATT_HEADS):
            alibi_s[head] = (_alibi_slope(head) * LOG2_E) * key_off

    cand = lax.broadcasted_iota(jnp.int32, (kmean_s.shape[0], blk), 0)
    valid = cand < i
    feat = lax.broadcasted_iota(jnp.int32, (pair, blk), 0)

    pair_lane = lax.broadcasted_iota(jnp.int32, (kmean_s.shape[0], pair), 1)
    for hp in range(N_ATT_HEADS // 2):
        q_raw = q_ref[hp * pair:(hp + 1) * pair, :]
        q2 = q_raw.astype(F32) * (HEAD_DIM ** -0.5 * LOG2_E)
        kmean_pair = kmean_s[:, hp * pair:(hp + 1) * pair]
        for sub in range(2):
            head = 2 * hp + sub
            in_head = (feat < HEAD_DIM) if sub == 0 else (feat >= HEAD_DIM)
            qm_s[head] = jnp.where(in_head, q2, 0.0).astype(BF16)

            km = jnp.where((pair_lane < HEAD_DIM) if sub == 0 else (pair_lane >= HEAD_DIM), kmean_pair, 0.0)
            km_hi = km.astype(BF16)
            rest = km - km_hi.astype(F32)
            km_mid = rest.astype(BF16)
            km_lo = (rest - km_mid.astype(F32)).astype(BF16)
            gate = (jnp.dot(km_lo, q_raw, preferred_element_type=F32)
                    + jnp.dot(km_mid, q_raw, preferred_element_type=F32)
                    + jnp.dot(km_hi, q_raw, preferred_element_type=F32))
            gate = jnp.where(valid, gate, NEG_INF)
            rank = jnp.zeros(gate.shape, F32)
            for jp in range(nb):
                gj = gate[jp:jp + 1, :]
                ge = jnp.where(gj >= gate, 1.0, 0.0)
                gt = jnp.where(gj > gate, 1.0, 0.0)
                rank = rank + jnp.where(cand > jp, ge, gt)
            sel_s[head] = jnp.where(valid, jnp.where(rank < MOBA_TOPK, 0.0, NEG_INF), NEG_INF)

    m_s[...] = jnp.full(m_s.shape, NEG_INF, F32)
    acc_s[...] = jnp.zeros(acc_s.shape, F32)

    causal = (lax.broadcasted_iota(jnp.int32, (blk, blk), 0)
              <= lax.broadcasted_iota(jnp.int32, (blk, blk), 1))

    all_heads = tuple(range(N_ATT_HEADS))

    def scores(tile, slot, heads=all_heads):
        for head in heads:
            cols = slice((head // 2) * pair, (head // 2 + 1) * pair)
            kj = kown_ref[:, cols] if tile is None else k_ref[tile * blk:(tile + 1) * blk, cols]
            t = jnp.dot(kj, qm_s[head], preferred_element_type=F32) + alibi_s[head]
            if tile is None:
                t = jnp.where(causal, t, NEG_INF)
            t_s[slot, head] = t
            quarter = blk // 4
            tq = jnp.maximum(jnp.maximum(t[0:quarter], t[quarter:2 * quarter]),
                             jnp.maximum(t[2 * quarter:3 * quarter], t[3 * quarter:]))
            tmax_s[slot, head, 0:1, :] = jnp.max(tq, axis=0, keepdims=True)

    ones_rows = jnp.ones((BF16_SUBLANES, blk), BF16)

    def accumulate(tile, slot, heads=all_heads):
        for head in heads:
            rows = slice(head * HEAD_DIM, (head + 1) * HEAD_DIM)
            m_old = m_s[head, 0:1, :]
            if tile is None:
                m_new = jnp.maximum(m_old, tmax_s[slot, head, 0:1, :])
                shift = m_new
                v = vown_ref[rows, :]
            else:
                row = (sel_s[head, tile:tile + 1, :]
                       + (_alibi_slope(head) * LOG2_E * blk) * (tile - i).astype(F32))
                m_new = jnp.maximum(m_old, tmax_s[slot, head, 0:1, :] + row)
                shift = m_new - row
                v = v_ref[rows, tile * blk:(tile + 1) * blk]
            p = jnp.exp2(t_s[slot, head] - shift)
            alpha = jnp.exp2(m_old - m_new)
            m_s[head, 0:1, :] = m_new
            vj = jnp.concatenate([v, ones_rows], axis=0)
            acc_s[head] = alpha * acc_s[head] + jnp.dot(vj, p.astype(BF16), preferred_element_type=F32)

    scores(None, 0)
    for j in range(nb - 1):
        @pl.when(j < i)
        def _(j=j):
            prev = None if j == 0 else j - 1
            scores(j, (j + 1) % 2, all_heads[:ATTN_SCORE_LEAD])
            for head in all_heads:
                accumulate(prev, j % 2, (head,))
                scores(j, (j + 1) % 2, all_heads[head + ATTN_SCORE_LEAD:head + ATTN_SCORE_LEAD + 1])

    @pl.when(i == 0)
    def _():
        accumulate(None, 0)

    for j in range(nb - 1):
        @pl.when(i == j + 1)
        def _(j=j):
            accumulate(j, (j + 1) % 2)

    for head in range(N_ATT_HEADS):
        out_s[head * HEAD_DIM:(head + 1) * HEAD_DIM, :] = (
            acc_s[head, 0:HEAD_DIM, :] / acc_s[head, HEAD_DIM:HEAD_DIM + 1, :])
    att = out_s[...].T
    o_ref[...] = _rms_rows(att, g_ref[...]).astype(BF16)


def _attention(feat_t, k_rm, attn_g, batch, seq):
    nb = seq // MOBA_BLOCK
    cands = max(SUBLANES, nb)
    q_blk = F_Q // D_ATT
    v_blk = F_V // D_ATT
    return pl.pallas_call(
        functools.partial(_attn_kernel, nb=nb),
        grid=(batch, nb),
        in_specs=[
            pl.BlockSpec((D_ATT, MOBA_BLOCK), lambda b, i: (q_blk, b * nb + i)),
            pl.BlockSpec((seq, D_ATT), lambda b, i: (b, 0)),
            pl.BlockSpec((MOBA_BLOCK, D_ATT), lambda b, i: (b * nb + i, 0)),
            pl.BlockSpec((D_ATT, seq), lambda b, i: (v_blk, b)),
            pl.BlockSpec((D_ATT, MOBA_BLOCK), lambda b, i: (v_blk, b * nb + i)),
            pl.BlockSpec((1, D_ATT), lambda b, i: (0, 0)),
        ],
        out_specs=pl.BlockSpec((MOBA_BLOCK, D_ATT), lambda b, i: (b * nb + i, 0)),
        out_shape=jax.ShapeDtypeStruct((batch * seq, D_ATT), BF16),
        scratch_shapes=[
            pltpu.VMEM((cands, D_ATT), F32),
            pltpu.VMEM((N_ATT_HEADS, 2 * HEAD_DIM, MOBA_BLOCK), BF16),
            pltpu.VMEM((N_ATT_HEADS, cands, MOBA_BLOCK), F32),
            pltpu.VMEM((N_ATT_HEADS, MOBA_BLOCK, MOBA_BLOCK), F32),
            pltpu.VMEM((2, N_ATT_HEADS, MOBA_BLOCK, MOBA_BLOCK), F32),
            pltpu.VMEM((2, N_ATT_HEADS, SUBLANES, MOBA_BLOCK), F32),
            pltpu.VMEM((N_ATT_HEADS, SUBLANES, MOBA_BLOCK), F32),
            pltpu.VMEM((N_ATT_HEADS, HEAD_DIM + BF16_SUBLANES, MOBA_BLOCK), F32),
            pltpu.VMEM((D_ATT, MOBA_BLOCK), F32),
        ],
        compiler_params=pltpu.CompilerParams(
            dimension_semantics=("arbitrary", "arbitrary"), vmem_limit_bytes=32 * MIB),
        name="moba_attention",
    )(feat_t, k_rm, k_rm, feat_t, feat_t, attn_g)


P_DSKIP, P_NORMG = 0, 1
N_XS_PARAMS = 2


def _ssd_kernel(zs_ref, xs_ref, b_ref, c_ref, dt_ref, pxs_ref, pdt_ref, y_ref,
                bt_s, cbt_s, ylocal_s, newst_s, state_s, *, seq):
    L = SSD_CHUNK
    n_chunks = seq // L
    r_i = lax.broadcasted_iota(jnp.int32, (L, L), 0)
    c_i = lax.broadcasted_iota(jnp.int32, (L, L), 1)
    upper = jnp.where(r_i <= c_i, 1.0, 0.0).astype(F32)
    s_le_l = r_i <= c_i

    dt_bias = pdt_ref[0]
    a_neg = -jnp.exp(pdt_ref[1])

    raw = jnp.concatenate([dt_ref[:, c * L:(c + 1) * L] + dt_bias for c in range(n_chunks)], axis=0)
    dt_all = jnp.maximum(raw, 0.0) + jnp.log1p(jnp.exp(-jnp.abs(raw)))
    a_dt = dt_all * jnp.concatenate([a_neg] * n_chunks, axis=0)
    acs_all = jnp.dot(a_dt, upper, precision=HIGHEST, preferred_element_type=F32)
    acs_log2 = acs_all * LOG2_E
    acs_log2_t = acs_log2.T
    acs_end = jnp.broadcast_to(acs_all[:, L - 1:L], acs_all.shape)
    to_end_all = jnp.exp(acs_end - acs_all)
    out_decay_all = jnp.exp(acs_all)
    carry_decay_all = jnp.exp(acs_end)

    def rows_of(c):
        return slice(c * DT_ROWS_PER_GROUP, (c + 1) * DT_ROWS_PER_GROUP)

    def head_rows(k):
        return slice(k * HEAD_DIM, (k + 1) * HEAD_DIM)

    for c in range(n_chunks):
        b_t = b_ref[:, c * L:(c + 1) * L].astype(F32).T.astype(BF16)
        bt_s[c] = b_t
        cbt_s[c] = jnp.dot(b_t, c_ref[:, c * L:(c + 1) * L], preferred_element_type=F32)

    for c in range(n_chunks):
        cols = slice(c * L, (c + 1) * L)
        xs = xs_ref[:, cols].astype(F32)
        dt = dt_all[rows_of(c)]
        acs_row = acs_log2[rows_of(c)]
        acs_col = acs_log2_t[:, rows_of(c)]
        to_end = to_end_all[rows_of(c)]
        cb_t = cbt_s[c]
        y_diag, x_state = [], []
        for k in range(HEADS_PER_GROUP):
            x_dt = xs[head_rows(k), :] * dt[k:k + 1, :]
            seg = acs_row[k:k + 1, :] - acs_col[:, k:k + 1]
            decay = jnp.exp2(jnp.where(s_le_l, seg, NEG_INF))
            m_t = (cb_t * decay).astype(BF16)
            y_diag.append(jnp.dot(x_dt.astype(BF16), m_t, preferred_element_type=F32))
            x_state.append((x_dt * to_end[k:k + 1, :]).astype(BF16))
        ylocal_s[:, cols] = jnp.concatenate(y_diag, axis=0) + pxs_ref[P_DSKIP] * xs
        newst_s[c] = jnp.dot(jnp.concatenate(x_state, axis=0), bt_s[c], preferred_element_type=F32)

    state_s[...] = jnp.zeros(state_s.shape, F32)
    for c in range(n_chunks):
        cols = slice(c * L, (c + 1) * L)
        out_decay = jnp.concatenate(
            [jnp.broadcast_to(out_decay_all[rows_of(c)][k:k + 1, :], (HEAD_DIM, L))
             for k in range(HEADS_PER_GROUP)], axis=0)
        carry_decay = jnp.concatenate(
            [jnp.broadcast_to(carry_decay_all[rows_of(c)][k:k + 1, :], (HEAD_DIM, L))
             for k in range(HEADS_PER_GROUP)], axis=0)
        prev = state_s[...]
        y_off = jnp.dot(prev.astype(BF16), c_ref[:, cols], preferred_element_type=F32) * out_decay
        state_s[...] = prev * carry_decay + newst_s[c]
        y = (ylocal_s[:, cols] + y_off) * zs_ref[:, cols].astype(F32)
        ms = jnp.mean(y * y, axis=0, keepdims=True)
        yn = (y * lax.rsqrt(ms + EPS)) * pxs_ref[P_NORMG]
        y_ref[cols, :] = yn.T.astype(BF16)


def _ssd(feat_t, dt_t, pxs, pdt, batch, seq):
    g_rows = GROUP_ROWS
    z_blk0 = F_Z // g_rows
    xs_blk0 = F_XS // g_rows
    b_blk0 = F_B // D_STATE
    c_blk0 = F_C // D_STATE
    return pl.pallas_call(
        functools.partial(_ssd_kernel, seq=seq),
        grid=(batch, N_SSM_GROUPS),
        in_specs=[
            pl.BlockSpec((g_rows, seq), lambda b, g: (z_blk0 + g, b)),
            pl.BlockSpec((g_rows, seq), lambda b, g: (xs_blk0 + g, b)),
            pl.BlockSpec((D_STATE, seq), lambda b, g: (b_blk0 + g, b)),
            pl.BlockSpec((D_STATE, seq), lambda b, g: (c_blk0 + g, b)),
            pl.BlockSpec((DT_ROWS_PER_GROUP, seq), lambda b, g: (g, b)),
            pl.BlockSpec((N_XS_PARAMS, g_rows, LANES), lambda b, g: (0, g, 0)),
            pl.BlockSpec((2, DT_ROWS_PER_GROUP, LANES), lambda b, g: (0, g, 0)),
        ],
        out_specs=pl.BlockSpec((seq, g_rows), lambda b, g: (b, g)),
        out_shape=jax.ShapeDtypeStruct((batch * seq, D_SSM), BF16),
        scratch_shapes=[
            pltpu.VMEM((seq // SSD_CHUNK, SSD_CHUNK, D_STATE), BF16),
            pltpu.VMEM((seq // SSD_CHUNK, SSD_CHUNK, SSD_CHUNK), F32),
            pltpu.VMEM((g_rows, seq), F32),
            pltpu.VMEM((seq // SSD_CHUNK, g_rows, D_STATE), F32),
            pltpu.VMEM((g_rows, D_STATE), F32),
        ],
        compiler_params=pltpu.CompilerParams(
            dimension_semantics=("arbitrary", "arbitrary"), vmem_limit_bytes=40 * MIB),
        name="ssd_mixer",
    )(feat_t, feat_t, feat_t, feat_t, dt_t, pxs, pdt)


OUTPROJ_TM = 1024


def _outproj_kernel(x_ref, a_ref, y_ref, wa_ref, wy_ref, o_ref):
    o_ref[...] = (x_ref[...]
                  + jnp.dot(a_ref[...], wa_ref[...], preferred_element_type=F32)
                  + jnp.dot(y_ref[...], wy_ref[...], preferred_element_type=F32))


def _outproj(x2, att, y, w_att, w_ssm):
    tokens = x2.shape[0]
    tm = OUTPROJ_TM
    resident = dict(pipeline_mode=pl.Buffered(1))
    return pl.pallas_call(
        _outproj_kernel,
        grid=(tokens // tm,),
        in_specs=[
            pl.BlockSpec((tm, D_MODEL), lambda i: (i, 0)),
            pl.BlockSpec((tm, D_ATT), lambda i: (i, 0)),
            pl.BlockSpec((tm, D_SSM), lambda i: (i, 0)),
            pl.BlockSpec((D_ATT, D_MODEL), lambda i: (0, 0), **resident),
            pl.BlockSpec((D_SSM, D_MODEL), lambda i: (0, 0), **resident),
        ],
        out_specs=pl.BlockSpec((tm, D_MODEL), lambda i: (i, 0)),
        out_shape=jax.ShapeDtypeStruct((tokens, D_MODEL), F32),
        compiler_params=pltpu.CompilerParams(
            dimension_semantics=("arbitrary",), vmem_limit_bytes=40 * MIB),
        name="outproj",
    )(x2, att, y, w_att, w_ssm)


FFN_TM = 512
FFN_TN = 256
FFN_HALO = BF16_SUBLANES
FFN_U_BUFFERS = 4
FFN_ROW_PIECES = 2


def _ffn_kernel(x_ref, g2_ref, wup_ref, cw_ref, cb_ref, wdn_ref, gf_ref, o_ref,
                h_s, u_s, act_s, *, tiles_per_seq):
    i = pl.program_id(0)
    tm = FFN_TM
    halo = FFN_HALO

    @pl.when(i % tiles_per_seq == 0)
    def _():
        h_s[0:halo, :] = jnp.zeros((halo, D_MODEL), BF16)

    x = x_ref[...]
    h_s[halo:, :] = _rms_rows(x, g2_ref[...]).astype(BF16)
    h_ext = h_s[...]

    def project(col0, width, buf):
        u_s[buf] = jnp.dot(h_ext, wup_ref[:, col0:col0 + width], preferred_element_type=F32)

    def conv(col0, width, buf, r0, rows):
        acc = cb_ref[:, col0:col0 + width]
        for k in range(FFN_CONV):
            lo = halo + r0 - (FFN_CONV - 1 - k)
            acc = acc + cw_ref[k:k + 1, col0:col0 + width] * u_s[buf, lo:lo + rows, :]
        return acc

    piece = tm // FFN_ROW_PIECES
    for c, n0 in enumerate(range(0, D_FF, FFN_TN)):
        gbuf, vbuf = (2 * c) % FFN_U_BUFFERS, (2 * c + 1) % FFN_U_BUFFERS
        project(n0, FFN_TN, gbuf)
        project(D_FF + n0, FFN_TN, vbuf)
        for r0 in range(0, tm, piece):
            gate = conv(n0, FFN_TN, gbuf, r0, piece)
            val = conv(D_FF + n0, FFN_TN, vbuf, r0, piece)
            act_s[r0:r0 + piece, n0:n0 + FFN_TN] = ((gate * _sigmoid(gate)) * val).astype(BF16)

    h_s[0:halo, :] = h_s[tm:tm + halo, :]
    half = (D_FF // (2 * LANES)) * LANES
    x2 = (x + jnp.dot(act_s[:, :half], wdn_ref[:half, :], preferred_element_type=F32)
          + jnp.dot(act_s[:, half:], wdn_ref[half:, :], preferred_element_type=F32))
    o_ref[...] = _rms_rows(x2, gf_ref[...])


def _ffn(x1, ln2_g, w_up, conv_w, conv_b, w_down, lnf_g, seq):
    tokens = x1.shape[0]
    tm = FFN_TM
    resident = dict(pipeline_mode=pl.Buffered(1))
    return pl.pallas_call(
        functools.partial(_ffn_kernel, tiles_per_seq=seq // tm),
        grid=(tokens // tm,),
        in_specs=[
            pl.BlockSpec((tm, D_MODEL), lambda i: (i, 0)),
            pl.BlockSpec((1, D_MODEL), lambda i: (0, 0)),
            pl.BlockSpec((D_MODEL, 2 * D_FF), lambda i: (0, 0), **resident),
            pl.BlockSpec((FFN_CONV, 2 * D_FF), lambda i: (0, 0)),
            pl.BlockSpec((1, 2 * D_FF), lambda i: (0, 0)),
            pl.BlockSpec((D_FF, D_MODEL), lambda i: (0, 0), **resident),
            pl.BlockSpec((1, D_MODEL), lambda i: (0, 0)),
        ],
        out_specs=pl.BlockSpec((tm, D_MODEL), lambda i: (i, 0)),
        out_shape=jax.ShapeDtypeStruct((tokens, D_MODEL), F32),
        scratch_shapes=[
            pltpu.VMEM((FFN_HALO + tm, D_MODEL), BF16),
            pltpu.VMEM((FFN_U_BUFFERS, FFN_HALO + tm, FFN_TN), F32),
            pltpu.VMEM((tm, D_FF), BF16),
        ],
        compiler_params=pltpu.CompilerParams(
            dimension_semantics=("arbitrary",), vmem_limit_bytes=48 * MIB),
        name="conv_ffn",
    )(x1, ln2_g, w_up, conv_w, conv_b, w_down, lnf_g)


def _lane_bcast(v):
    return jnp.broadcast_to(v[..., None], v.shape + (LANES,)).astype(F32)


def _group_pad(v):
    v = v.reshape(N_SSM_GROUPS, HEADS_PER_GROUP)
    v = jnp.pad(v, ((0, 0), (0, DT_ROWS_PER_GROUP - HEADS_PER_GROUP)))
    return v.reshape(N_SSM_GROUPS * DT_ROWS_PER_GROUP)


def _layer(x2, batch, seq, ln1_g, w_in, ssm_conv_w, ssm_conv_b, dt_bias, a_log, d_skip,
           attn_norm_g, ssm_norm_g, w_out, ln2_g, w_up, ffn_conv_w, ffn_conv_b, w_down, out_g):
    c_q, c_k, c_v, c_z = 0, D_ATT, 2 * D_ATT, 3 * D_ATT
    c_xs = c_z + D_SSM
    c_b = c_xs + D_SSM
    c_c = c_b + N_SSM_GROUPS * D_STATE
    c_dt = c_c + N_SSM_GROUPS * D_STATE

    w_k = w_in[:, c_k:c_v].astype(BF16)
    w_f_t = jnp.concatenate(
        [w_in[:, c_z:c_xs], w_in[:, c_xs:c_b], w_in[:, c_q:c_k], w_in[:, c_v:c_z],
         w_in[:, c_b:c_c], w_in[:, c_c:c_dt]], axis=1).T.astype(BF16)
    w_dt_t = w_in[:, c_dt:].T.reshape(N_SSM_GROUPS, HEADS_PER_GROUP, D_MODEL)
    w_dt_t = jnp.pad(w_dt_t, ((0, 0), (0, DT_ROWS_PER_GROUP - HEADS_PER_GROUP), (0, 0)))
    w_dt_t = w_dt_t.reshape(N_SSM_GROUPS * DT_ROWS_PER_GROUP, D_MODEL).astype(BF16)

    conv_p = _lane_bcast(jnp.concatenate([ssm_conv_w, ssm_conv_b[None, :]], axis=0))
    k_rm, feat_t, dt_t = _inproj(x2, ln1_g[None, :], w_k, w_f_t, w_dt_t, conv_p, seq)

    att = _attention(feat_t, k_rm, attn_norm_g[None, :], batch, seq)

    pxs = _lane_bcast(jnp.stack([jnp.repeat(d_skip, HEAD_DIM), ssm_norm_g], axis=0))
    pdt = _lane_bcast(jnp.stack([_group_pad(dt_bias), _group_pad(a_log)], axis=0))
    y = _ssd(feat_t, dt_t, pxs, pdt, batch, seq)

    w_out_b = w_out.astype(BF16)
    x1 = _outproj(x2, att, y, w_out_b[:D_ATT], w_out_b[D_ATT:])

    return _ffn(x1, ln2_g[None, :], w_up.astype(BF16), ffn_conv_w, ffn_conv_b[None, :],
                w_down.astype(BF16), out_g[None, :], seq)


def kernel(x, ln1_g, w_in, ssm_conv_w, ssm_conv_b, dt_bias, a_log, d_skip, attn_norm_g,
           ssm_norm_g, w_out, ln2_g, w_up, ffn_conv_w, ffn_conv_b, w_down, lnf_g):
    batch, seq, d_model = x.shape
    depth = ln1_g.shape[0]
    assert d_model == D_MODEL and depth == 1
    assert seq % MOBA_BLOCK == 0 and seq % FFN_TM == 0 and seq % INPROJ_TM == 0
    assert (batch * seq) % OUTPROJ_TM == 0
    x2 = x.reshape(batch * seq, d_model)
    out = _layer(x2, batch, seq, ln1_g[0], w_in[0], ssm_conv_w[0], ssm_conv_b[0], dt_bias[0],
                 a_log[0], d_skip[0], attn_norm_g[0], ssm_norm_g[0], w_out[0], ln2_g[0],
                 w_up[0], ffn_conv_w[0], ffn_conv_b[0], w_down[0], lnf_g)
    return out.reshape(batch, seq, d_model)
```
